```python
import jax, jax.numpy as jnp
from jax import lax
import numpy as np

D_MODEL = 2048
BATCH = 2
SEQ = 4096
DEPTH = 4
DEC_BATCH = 8
DEC_SEQ = 1
PAST_LEN = 16384
PAGE_SIZE = 128

HEAD_DIM = 128
NSA_HEADS = 8
NSA_KV_GROUPS = 2
HEADS_PER_GROUP = NSA_HEADS // NSA_KV_GROUPS
SB_HEADS = 8
CMP_BLOCK = 32
CMP_STRIDE = 16
SEL_BLOCK = 64
N_SELECT = 16
WINDOW = 512
ROPE_THETA = 10000.0
PLE_DIM = 256
Q_BLOCK = 128
RMS_EPS = 1e-6
SEL_BONUS = 1e3
NEG_BIG = -1e30

NSA_W = NSA_HEADS * HEAD_DIM
KV_W = NSA_KV_GROUPS * HEAD_DIM
SB_W = SB_HEADS * HEAD_DIM
SPLITS = (NSA_W, 6 * KV_W, NSA_HEADS * 3, NSA_W, 3 * SB_W, SB_W, 2 * D_MODEL)
IN_COLS = sum(SPLITS)

kernel_name = "nsa_stickbreaking_hybrid_step"


def rms_norm(x, g):
    x32 = x.astype(jnp.float32)
    y = x32 * lax.rsqrt(jnp.mean(x32 * x32, axis=-1, keepdims=True) + RMS_EPS)
    return (y * g.astype(jnp.float32)).astype(x.dtype)


def rope(x, pos):
    half = HEAD_DIM // 2
    inv = ROPE_THETA ** (-jnp.arange(half, dtype=jnp.float32) / half)
    ang = pos.astype(jnp.float32)[:, None] * inv[None, :]
    cos = jnp.cos(ang)[None, :, None, :]
    sin = jnp.sin(ang)[None, :, None, :]
    x32 = x.astype(jnp.float32)
    x1, x2 = x32[..., :half], x32[..., half:]
    return jnp.concatenate([x1 * cos - x2 * sin, x2 * cos + x1 * sin], axis=-1).astype(x.dtype)


def masked_softmax(s, mask):
    p = jax.nn.softmax(jnp.where(mask, s, NEG_BIG), axis=-1)
    return jnp.where(mask, p, 0.0)


def query_block_size(tq):
    return Q_BLOCK if tq % Q_BLOCK == 0 else tq


def to_blocks(a, qb):
    b, t = a.shape[:2]
    return jnp.moveaxis(a.reshape(b, t // qb, qb, *a.shape[2:]), 1, 0)


def from_blocks(a):
    nb, b, qb = a.shape[:3]
    return jnp.moveaxis(a, 0, 1).reshape(b, nb * qb, *a.shape[3:])


def compress(rows, pos_emb, w1, b1, w2):
    t = rows.shape[1]
    n_cmp = (t - CMP_BLOCK) // CMP_STRIDE + 1
    idx = np.arange(n_cmp)[:, None] * CMP_STRIDE + np.arange(CMP_BLOCK)[None, :]
    blk = rows[:, idx].astype(jnp.float32) + pos_emb[None, None, :, None, :]
    hid = jax.nn.silu(jnp.einsum('bnlgd,lde->bnge', blk, w1) + b1)
    return jnp.einsum('bnge,ef->bngf', hid, w2)


def nsa_attend(q_n, q_r, gates, k_cmp, v_cmp, k_sel, v_sel, k_win_pad, v_win_pad,
               q_off, cmp_pos, w_c1, b_c1, w_c2):
    b, tq = q_n.shape[:2]
    t_all = k_cmp.shape[1]
    scale = HEAD_DIM ** -0.5
    ck = compress(k_cmp, cmp_pos[0], w_c1[0], b_c1[0], w_c2[0])
    cv = compress(v_cmp, cmp_pos[1], w_c1[1], b_c1[1], w_c2[1])
    n_cmp = ck.shape[1]
    cmp_start = np.arange(n_cmp) * CMP_STRIDE
    cmp_end = jnp.asarray(cmp_start + CMP_BLOCK - 1)
    n_sel = -(-t_all // SEL_BLOCK)
    sel_start = np.arange(n_sel) * SEL_BLOCK
    overlap = jnp.asarray(((cmp_start[:, None] < sel_start[None, :] + SEL_BLOCK)
                           & (cmp_start[:, None] + CMP_BLOCK > sel_start[None, :])).astype(np.float32))
    k_top = min(N_SELECT, n_sel)
    pad = n_sel * SEL_BLOCK - t_all

    def sel_blocks(a):
        a = jnp.pad(a.astype(jnp.float32), ((0, 0), (0, pad), (0, 0), (0, 0)))
        return a.reshape(b, n_sel, SEL_BLOCK, NSA_KV_GROUPS, HEAD_DIM).transpose(0, 3, 1, 2, 4)

    ks_b, vs_b = sel_blocks(k_sel), sel_blocks(v_sel)
    kw_all = k_win_pad.astype(jnp.float32)
    vw_all = v_win_pad.astype(jnp.float32)
    qb = query_block_size(tq)
    nb = tq // qb
    bi = jnp.arange(b)[:, None, None]
    gi = jnp.arange(NSA_KV_GROUPS)[None, :, None]
    blk_ids = jnp.arange(n_sel)

    def block(args):
        j, qn, qr, g = args
        t = q_off + j * qb + jnp.arange(qb)
        qn = qn.astype(jnp.float32).reshape(b, qb, NSA_KV_GROUPS, HEADS_PER_GROUP, HEAD_DIM)
        qr = qr.astype(jnp.float32).reshape(b, qb, NSA_KV_GROUPS, HEADS_PER_GROUP, HEAD_DIM)
        s = jnp.einsum('bqghd,bngd->bghqn', qn, ck) * scale
        p_c = masked_softmax(s, cmp_end[None, :] <= t[:, None])
        o_c = jnp.einsum('bghqn,bngd->bqghd', p_c, cv)
        imp = jnp.einsum('bghqn,ns->bgqs', p_c, overlap)
        tb = t // SEL_BLOCK
        forced = ((blk_ids[None, :] == 0) | (blk_ids[None, :] == tb[:, None])
                  | (blk_ids[None, :] == tb[:, None] - 1))
        future = blk_ids[None, :] * SEL_BLOCK > t[:, None]
        score = jnp.where(future, NEG_BIG, imp + SEL_BONUS * forced.astype(jnp.float32))
        _, top = lax.top_k(score, k_top)
        top_flat = top.reshape(b, NSA_KV_GROUPS, qb * k_top)
        kg = ks_b[bi, gi, top_flat].reshape(b, NSA_KV_GROUPS, qb, k_top * SEL_BLOCK, HEAD_DIM)
        vg = vs_b[bi, gi, top_flat].reshape(b, NSA_KV_GROUPS, qb, k_top * SEL_BLOCK, HEAD_DIM)
        tok = (top[..., None] * SEL_BLOCK + jnp.arange(SEL_BLOCK)).reshape(
            b, NSA_KV_GROUPS, qb, k_top * SEL_BLOCK)
        s = jnp.einsum('bqghd,bgqkd->bghqk', qr, kg) * scale
        p_s = masked_softmax(s, (tok <= t[None, None, :, None])[:, :, None])
        o_s = jnp.einsum('bghqk,bgqkd->bqghd', p_s, vg)
        kw = lax.dynamic_slice_in_dim(kw_all, j * qb, WINDOW + qb, axis=1)
        vw = lax.dynamic_slice_in_dim(vw_all, j * qb, WINDOW + qb, axis=1)
        pw = q_off - WINDOW + j * qb + jnp.arange(WINDOW + qb)
        dt = t[:, None] - pw[None, :]
        wmask = (pw[None, :] >= 0) & (dt >= 0) & (dt < WINDOW)
        s = jnp.einsum('bqghd,bkgd->bghqk', qr, kw) * scale
        p_w = masked_softmax(s, wmask)
        o_w = jnp.einsum('bghqk,bkgd->bqghd', p_w, vw)
        gg = jax.nn.sigmoid(g.astype(jnp.float32)).reshape(b, qb, NSA_KV_GROUPS, HEADS_PER_GROUP, 3)
        o = gg[..., 0:1] * o_c + gg[..., 1:2] * o_s + gg[..., 2:3] * o_w
        return o.reshape(b, qb, NSA_HEADS, HEAD_DIM).astype(q_n.dtype)

    out = lax.map(block, (jnp.arange(nb), to_blocks(q_n, qb), to_blocks(q_r, qb), to_blocks(gates, qb)))
    return from_blocks(out)


def sb_attend(q, k, v, q_off):
    b, tq = q.shape[:2]
    t_all = k.shape[1]
    scale = HEAD_DIM ** -0.5
    k32 = k.astype(jnp.float32)
    v32 = v.astype(jnp.float32)
    kpos = jnp.arange(t_all)
    qb = query_block_size(tq)
    nb = tq // qb

    def block(args):
        j, qblk = args
        t = q_off + j * qb + jnp.arange(qb)
        z = jnp.einsum('bqhd,bkhd->bhqk', qblk.astype(jnp.float32), k32) * scale
        mask = kpos[None, :] < t[:, None]
        log1m = jnp.where(mask, jax.nn.log_sigmoid(-z), 0.0)
        after = lax.cumsum(log1m, axis=3, reverse=True) - log1m
        a = jnp.where(mask, jnp.exp(jax.nn.log_sigmoid(z) + after), 0.0)
        return jnp.einsum('bhqk,bkhd->bqhd', a, v32).astype(q.dtype)

    out = lax.map(block, (jnp.arange(nb), to_blocks(q, qb)))
    return from_blocks(out)


def trunk_layer(x, p_l, q_off, past_nsa, past_sb, win_buf, g_mix, w_in, g_q, g_k,
                cmp_pos, w_c1, b_c1, w_c2, w_up_a, w_up_b, w_out, g_ple, w_pg, w_ple):
    b, t = x.shape[:2]
    xn = rms_norm(x, g_mix)
    z = xn @ w_in
    q_a, kv_a, ga, za, qkv_b, zb, m = jnp.split(z, list(np.cumsum(SPLITS)[:-1]), axis=-1)
    pos = q_off + jnp.arange(t)
    q_a = q_a.reshape(b, t, NSA_HEADS, HEAD_DIM)
    kv_a = kv_a.reshape(b, t, 6, NSA_KV_GROUPS, HEAD_DIM)
    q_n = rms_norm(q_a, g_q)
    q_r = rope(q_n, pos)
    kc = rms_norm(kv_a[:, :, 0], g_k[0])
    ks = rope(rms_norm(kv_a[:, :, 2], g_k[1]), pos)
    kw = rope(rms_norm(kv_a[:, :, 4], g_k[2]), pos)
    nsa_rows = jnp.stack([kc, kv_a[:, :, 1], ks, kv_a[:, :, 3]], axis=2)
    win_rows = jnp.stack([kw, kv_a[:, :, 5]], axis=2)
    qkv_b = qkv_b.reshape(b, t, 3, SB_HEADS, HEAD_DIM)
    q_b = qkv_b[:, :, 0]
    sb_rows = qkv_b[:, :, 1:3]
    if past_nsa is None:
        nsa_all, sb_all = nsa_rows, sb_rows
        win_all = jnp.pad(win_rows, ((0, 0), (WINDOW, 0), (0, 0), (0, 0), (0, 0)))
        new_win = win_rows[:, t - min(WINDOW, t):]
    else:
        nsa_all = jnp.concatenate([past_nsa.astype(x.dtype), nsa_rows], axis=1)
        sb_all = jnp.concatenate([past_sb.astype(x.dtype), sb_rows], axis=1)
        w_buf = win_buf.shape[1]
        cat = jnp.concatenate([win_buf.astype(x.dtype), win_rows], axis=1)
        new_win = cat[:, t:]
        win_all = jnp.pad(cat, ((0, 0), (WINDOW - w_buf, 0), (0, 0), (0, 0), (0, 0)))
    o_a = nsa_attend(q_n, q_r, ga.reshape(b, t, NSA_HEADS, 3),
                     nsa_all[:, :, 0], nsa_all[:, :, 1], nsa_all[:, :, 2], nsa_all[:, :, 3],
                     win_all[:, :, 0], win_all[:, :, 1], q_off, cmp_pos, w_c1, b_c1, w_c2)
    o_b = sb_attend(q_b, sb_all[:, :, 0], sb_all[:, :, 1], q_off)
    y_a = (o_a.reshape(b, t, NSA_W) * jax.nn.silu(za)) @ w_up_a
    y_b = (o_b.reshape(b, t, SB_W) * jax.nn.silu(zb)) @ w_up_b
    mg = jax.nn.sigmoid(m.reshape(b, t, 2, D_MODEL))
    h = x + (mg[:, :, 0] * y_a + mg[:, :, 1] * y_b) @ w_out
    h = h + jax.nn.sigmoid(rms_norm(h, g_ple) @ w_pg) * (p_l.astype(h.dtype) @ w_ple)
    return h, nsa_rows, sb_rows, new_win


def setup_inputs(seed: int = 0) -> dict:
    key = jax.random.key(seed)
    ks = jax.random.split(key, 24)
    f32 = jnp.float32
    nrm = jax.random.normal
    n_pages = PAST_LEN // PAGE_SIZE
    n_pool = (5 * DEC_BATCH * n_pages) // 4
    w_buf = min(WINDOW, PAST_LEN)
    x_prompt = nrm(ks[0], (BATCH, SEQ, D_MODEL), f32)
    x_sample = nrm(ks[1], (DEC_BATCH, DEC_SEQ, D_MODEL), f32)
    cache_nsa = nrm(ks[2], (DEPTH, n_pool, PAGE_SIZE, 4, NSA_KV_GROUPS, HEAD_DIM), f32)
    cache_sb = nrm(ks[3], (DEPTH, n_pool, PAGE_SIZE, 2, SB_HEADS, HEAD_DIM), f32)
    state_win = nrm(ks[4], (DEPTH, DEC_BATCH, w_buf, 2, NSA_KV_GROUPS, HEAD_DIM), f32)
    page_table = jax.random.permutation(ks[5], n_pool)[:DEC_BATCH * n_pages].reshape(
        DEC_BATCH, n_pages).astype(jnp.int32)
    p_prompt = nrm(ks[6], (DEPTH, BATCH, SEQ, PLE_DIM), f32)
    p_sample = nrm(ks[7], (DEPTH, DEC_BATCH, DEC_SEQ, PLE_DIM), f32)
    g_mix = 1.0 + 0.02 * nrm(ks[8], (DEPTH, D_MODEL), f32)
    w_in = nrm(ks[9], (DEPTH, D_MODEL, IN_COLS), f32) * D_MODEL ** -0.5
    g_q = 1.0 + 0.02 * nrm(ks[10], (DEPTH, HEAD_DIM), f32)
    g_k = 1.0 + 0.02 * nrm(ks[11], (DEPTH, 3, HEAD_DIM), f32)
    cmp_pos = 0.1 * nrm(ks[12], (DEPTH, 2, CMP_BLOCK, HEAD_DIM), f32)
    w_c1 = nrm(ks[13], (DEPTH, 2, CMP_BLOCK, HEAD_DIM, HEAD_DIM), f32) * (CMP_BLOCK * HEAD_DIM) ** -0.5
    b_c1 = 0.01 * nrm(ks[14], (DEPTH, 2, HEAD_DIM), f32)
    w_c2 = nrm(ks[15], (DEPTH, 2, HEAD_DIM, HEAD_DIM), f32) * HEAD_DIM ** -0.5
    w_up_a = nrm(ks[16], (DEPTH, NSA_W, D_MODEL), f32) * NSA_W ** -0.5
    w_up_b = nrm(ks[17], (DEPTH, SB_W, D_MODEL), f32) * SB_W ** -0.5
    w_out = nrm(ks[18], (DEPTH, D_MODEL, D_MODEL), f32) * D_MODEL ** -0.5
    g_ple = 1.0 + 0.02 * nrm(ks[19], (DEPTH, D_MODEL), f32)
    w_pg = nrm(ks[20], (DEPTH, D_MODEL, D_MODEL), f32) * D_MODEL ** -0.5
    w_ple = nrm(ks[21], (DEPTH, PLE_DIM, D_MODEL), f32) * PLE_DIM ** -0.5
    return {"x_prompt": x_prompt, "x_sample": x_sample, "cache_nsa": cache_nsa,
            "cache_sb": cache_sb, "state_win": state_win, "page_table": page_table,
            "p_prompt": p_prompt, "p_sample": p_sample, "g_mix": g_mix, "w_in": w_in,
            "g_q": g_q, "g_k": g_k, "cmp_pos": cmp_pos, "w_c1": w_c1, "b_c1": b_c1,
            "w_c2": w_c2, "w_up_a": w_up_a, "w_up_b": w_up_b, "w_out": w_out,
            "g_ple": g_ple, "w_pg": w_pg, "w_ple": w_ple}


def reference(x_prompt, x_sample, cache_nsa, cache_sb, state_win, page_table, p_prompt,
              p_sample, g_mix, w_in, g_q, g_k, cmp_pos, w_c1, b_c1, w_c2, w_up_a, w_up_b,
              w_out, g_ple, w_pg, w_ple):
    dec_b = x_sample.shape[0]
    past_len = page_table.shape[1] * PAGE_SIZE
    hp, hs = x_prompt, x_sample
    nsa_p, nsa_s, sb_p, sb_s, win_p, win_s = [], [], [], [], [], []
    for l in range(DEPTH):
        lw = (g_mix[l], w_in[l], g_q[l], g_k[l], cmp_pos[l], w_c1[l], b_c1[l], w_c2[l],
              w_up_a[l], w_up_b[l], w_out[l], g_ple[l], w_pg[l], w_ple[l])
        hp, rn, rs, rw = trunk_layer(hp, p_prompt[l], 0, None, None, None, *lw)
        nsa_p.append(rn); sb_p.append(rs); win_p.append(rw)
        past_nsa = cache_nsa[l][page_table].reshape(dec_b, past_len, 4, NSA_KV_GROUPS, HEAD_DIM)
        past_sb = cache_sb[l][page_table].reshape(dec_b, past_len, 2, SB_HEADS, HEAD_DIM)
        hs, rn, rs, rw = trunk_layer(hs, p_sample[l], past_len, past_nsa, past_sb, state_win[l], *lw)
        nsa_s.append(rn); sb_s.append(rs); win_s.append(rw)
    return (hp, hs, jnp.stack(nsa_p), jnp.stack(nsa_s), jnp.stack(sb_p), jnp.stack(sb_s),
            jnp.stack(win_p), jnp.stack(win_s))
```

```python
import functools

import numpy as np
import jax
import jax.numpy as jnp
from jax import lax
from jax.experimental import pallas as pl
from jax.experimental.pallas import tpu as pltpu

HEAD_DIM = 128
NSA_HEADS = 8
NSA_KV_GROUPS = 2
HEADS_PER_GROUP = NSA_HEADS // NSA_KV_GROUPS
SB_HEADS = 8
CMP_BLOCK = 32
CMP_STRIDE = 16
SEL_BLOCK = 64
N_SELECT = 16
WINDOW = 512
ROPE_THETA = 10000.0
PAGE_SIZE = 128
RMS_EPS = 1e-6
SEL_BONUS = 1e3
NEG_BIG = -1e30
BELOW_NEG_BIG = -3e38

NSA_W = NSA_HEADS * HEAD_DIM
KV_W = NSA_KV_GROUPS * HEAD_DIM
SB_W = SB_HEADS * HEAD_DIM
LANES = 128
SCALE = HEAD_DIM ** -0.5
V7X_VMEM_BYTES = 64 * 1024 * 1024

F32 = jnp.float32
BF16 = jnp.bfloat16

OFF_QA = 0
OFF_ZA = NSA_W
OFF_ZB = OFF_ZA + NSA_W
OFF_QB = OFF_ZB + SB_W
OFF_KB = OFF_QB + SB_W
OFF_VB = OFF_KB + SB_W
OFF_M0 = OFF_VB + SB_W
GA_W = 512


def _offsets(d_model):
    off_m1 = OFF_M0 + d_model
    off_kv = off_m1 + d_model
    off_ga = off_kv + 6 * KV_W
    return off_m1, off_kv, off_ga, off_ga + GA_W


def _params(sem, vmem_mb):
    return pltpu.CompilerParams(dimension_semantics=sem,
                                vmem_limit_bytes=min(vmem_mb * 1024 * 1024, V7X_VMEM_BYTES - (4 << 20)))


def _dot(a, b):
    return jnp.dot(a, b, preferred_element_type=F32)


def _dot_nt(a, b):
    return lax.dot_general(a, b, (((1,), (1,)), ((), ())), preferred_element_type=F32)


def _split3(a):
    hi = a.astype(BF16)
    r1 = a - hi.astype(F32)
    mid = r1.astype(BF16)
    lo = (r1 - mid.astype(F32)).astype(BF16)
    return hi, mid, lo


def _iota(shape, dim):
    return lax.broadcasted_iota(jnp.int32, shape, dim)


def _silu(x):
    return x * jax.nn.sigmoid(x)


def _softplus(z):
    return jnp.maximum(z, 0.0) + jnp.log1p(jnp.exp(-jnp.abs(z)))


def _masked_softmax(s, mask):
    s = jnp.where(mask, s, NEG_BIG)
    m = jnp.max(s, axis=-1, keepdims=True)
    p = jnp.where(mask, jnp.exp(s - m), 0.0)
    l = jnp.sum(p, axis=-1, keepdims=True)
    return p / jnp.where(l > 0.0, l, 1.0)


def _const_spec(block, index):
    return pl.BlockSpec(block, lambda *_: index)


def _proj_body(x_ref, g_ref, w_ref, o_ref, xn_ref):
    @pl.when(pl.program_id(1) == 0)
    def _():
        x = x_ref[...]
        ms = jnp.mean(x * x, axis=-1, keepdims=True)
        xn_ref[...] = (x * lax.rsqrt(ms + RMS_EPS) * g_ref[...]).astype(BF16)

    o_ref[...] = _dot(xn_ref[...], w_ref[...])


def _proj(x, g, w, l):
    n, d = x.shape
    c = w.shape[2]
    tm = min(n, 1024)
    tn = 512
    return pl.pallas_call(
        _proj_body,
        grid=(n // tm, c // tn),
        in_specs=[pl.BlockSpec((tm, d), lambda i, j: (i, 0)),
                  pl.BlockSpec((1, d), lambda i, j: (0, 0)),
                  pl.BlockSpec((None, d, tn), lambda i, j: (l, 0, j))],
        out_specs=pl.BlockSpec((tm, tn), lambda i, j: (i, j)),
        out_shape=jax.ShapeDtypeStruct((n, c), F32),
        scratch_shapes=[pltpu.VMEM((tm, d), BF16)],
        compiler_params=_params(("parallel", "arbitrary"), 48),
        name="proj",
    )(x, g, w)


def _prep_body(qa_ref, kc_ref, ks_ref, kw_ref, cos_ref, sin_ref, gq_ref, gk_ref,
               qn_ref, qr_ref, nsa_ref, win_ref):
    cos = cos_ref[...]
    sin = sin_ref[...]

    def norm(v, g):
        return v * lax.rsqrt(jnp.mean(v * v, axis=-1, keepdims=True) + RMS_EPS) * g

    def rope(v):
        return v * cos + pltpu.roll(v, HEAD_DIM // 2, 1) * sin

    gq = gq_ref[...]
    for h in range(NSA_HEADS):
        sl = slice(h * HEAD_DIM, (h + 1) * HEAD_DIM)
        qn = norm(qa_ref[:, sl], gq)
        qn_ref[:, sl] = qn.astype(BF16)
        qr_ref[:, sl] = rope(qn).astype(BF16)
    for g in range(NSA_KV_GROUPS):
        k_sl = slice(g * HEAD_DIM, (g + 1) * HEAD_DIM)
        v_sl = slice(KV_W + g * HEAD_DIM, KV_W + (g + 1) * HEAD_DIM)
        nsa_ref[:, k_sl] = norm(kc_ref[:, k_sl], gk_ref[0:1, :])
        nsa_ref[:, v_sl] = kc_ref[:, v_sl]
        nsa_ref[:, 2 * KV_W + g * HEAD_DIM:2 * KV_W + (g + 1) * HEAD_DIM] = rope(
            norm(ks_ref[:, k_sl], gk_ref[1:2, :]))
        nsa_ref[:, 3 * KV_W + g * HEAD_DIM:3 * KV_W + (g + 1) * HEAD_DIM] = ks_ref[:, v_sl]
        win_ref[:, k_sl] = rope(norm(kw_ref[:, k_sl], gk_ref[2:3, :]))
        win_ref[:, v_sl] = kw_ref[:, v_sl]


def _prep(z, cos, sin, gq, gk, off_kv):
    n = z.shape[0]
    tm = min(n, 256)
    nt = cos.shape[0] // tm
    kvb = off_kv // (2 * KV_W)
    return pl.pallas_call(
        _prep_body,
        grid=(n // tm,),
        in_specs=[pl.BlockSpec((tm, NSA_W), lambda i: (i, OFF_QA // NSA_W)),
                  pl.BlockSpec((tm, 2 * KV_W), lambda i: (i, kvb)),
                  pl.BlockSpec((tm, 2 * KV_W), lambda i: (i, kvb + 1)),
                  pl.BlockSpec((tm, 2 * KV_W), lambda i: (i, kvb + 2)),
                  pl.BlockSpec((tm, HEAD_DIM), lambda i: (i % nt, 0)),
                  pl.BlockSpec((tm, HEAD_DIM), lambda i: (i % nt, 0)),
                  pl.BlockSpec((1, HEAD_DIM), lambda i: (0, 0)),
                  pl.BlockSpec((3, HEAD_DIM), lambda i: (0, 0))],
        out_specs=[pl.BlockSpec((tm, NSA_W), lambda i: (i, 0)),
                   pl.BlockSpec((tm, NSA_W), lambda i: (i, 0)),
                   pl.BlockSpec((tm, 4 * KV_W), lambda i: (i, 0)),
                   pl.BlockSpec((tm, 2 * KV_W), lambda i: (i, 0))],
        out_shape=[jax.ShapeDtypeStruct((n, NSA_W), BF16),
                   jax.ShapeDtypeStruct((n, NSA_W), BF16),
                   jax.ShapeDtypeStruct((n, 4 * KV_W), F32),
                   jax.ShapeDtypeStruct((n, 2 * KV_W), F32)],
        compiler_params=_params(("parallel",), 32),
        name="prep",
    )(z, z, z, z, cos, sin, gq, gk)


def _rope_tables(pos):
    half = HEAD_DIM // 2
    inv = ROPE_THETA ** (-jnp.arange(half, dtype=F32) / half)
    ang = pos.astype(F32)[:, None] * inv[None, :]
    cos = jnp.cos(ang)
    sin = jnp.sin(ang)
    return jnp.concatenate([cos, cos], axis=-1), jnp.concatenate([-sin, sin], axis=-1)


def _cmp_rows(ref, nr, col0):
    return jnp.concatenate(
        [ref[pl.ds(l, nr, stride=CMP_STRIDE), col0:col0 + HEAD_DIM] for l in range(CMP_STRIDE)], axis=1)


def _cmp_hidden(p, q, b1):
    return _silu(p + pltpu.roll(q, q.shape[0] - 1, 0) + b1)


def _compress_body(rows_ref, pos_ref, w1_ref, b1_ref, w2_ref, o_ref, *, nr):
    r = _cmp_rows(rows_ref, nr, 0)
    half = CMP_STRIDE * HEAD_DIM
    p = _dot((r + pos_ref[0:1, :]).astype(BF16), w1_ref[0:half, :])
    q = _dot((r + pos_ref[1:2, :]).astype(BF16), w1_ref[half:2 * half, :])
    hid = _cmp_hidden(p, q, b1_ref[...])
    o_ref[...] = _dot(hid.astype(BF16), w2_ref[...])


def _compress(nsa_rows, b, t, pos, w1, b1, w2, l):
    nr = t // CMP_STRIDE
    half = CMP_STRIDE * HEAD_DIM
    return pl.pallas_call(
        functools.partial(_compress_body, nr=nr),
        grid=(b, NSA_KV_GROUPS, 2),
        in_specs=[pl.BlockSpec((t, HEAD_DIM), lambda i, g, k: (i, k * NSA_KV_GROUPS + g)),
                  pl.BlockSpec((None, None, 2, half), lambda i, g, k: (l, k, 0, 0)),
                  pl.BlockSpec((None, None, 2 * half, HEAD_DIM), lambda i, g, k: (l, k, 0, 0)),
                  pl.BlockSpec((None, None, 1, HEAD_DIM), lambda i, g, k: (l, k, 0, 0)),
                  pl.BlockSpec((None, None, HEAD_DIM, HEAD_DIM), lambda i, g, k: (l, k, 0, 0))],
        out_specs=pl.BlockSpec((None, None, None, nr, HEAD_DIM), lambda i, g, k: (i, g, k, 0, 0)),
        out_shape=jax.ShapeDtypeStruct((b, NSA_KV_GROUPS, 2, nr, HEAD_DIM), F32),
        compiler_params=_params(("parallel", "parallel", "parallel"), 32),
        name="compress",
    )(nsa_rows, pos, w1, b1, w2)


def _block_scores(imp, blk, t, n_sel):
    tb = t >> (SEL_BLOCK.bit_length() - 1)
    forced = (blk == 0) | (blk == tb) | (blk == tb - 1)
    future = blk * SEL_BLOCK > t
    score = jnp.where(future, NEG_BIG, imp + SEL_BONUS * jnp.where(forced, 1.0, 0.0))
    return jnp.where(blk < n_sel, score, BELOW_NEG_BIG)


def _nsa_body(qn_ref, qr_ref, ga_ref, ck_ref, cv_ref, ks_ref, vs_ref, kw_ref, vw_ref, ovt_ref, et_ref,
              o_ref, sc_ref, *, tq, tk, t_all, n_cmp, n_sel, k_top, wlen):
    j = pl.program_id(2)
    t0 = j * tq
    t_col = t0 + _iota((tq, 1), 0)
    t_row = t0 + _iota((1, tq), 1)
    nc = ck_ref.shape[0]
    nsp = ovt_ref.shape[0]

    ck = ck_ref[...].astype(BF16)
    cv = cv_ref[...].astype(BF16)
    n_idx = _iota((1, nc), 1)
    cmask = (n_idx * CMP_STRIDE + (CMP_BLOCK - 1) <= t_col) & (n_idx < n_cmp)
    psum = jnp.zeros((tq, nc), F32)
    o_c = []
    for h in range(HEADS_PER_GROUP):
        q = qn_ref[:, h * HEAD_DIM:(h + 1) * HEAD_DIM]
        p = _masked_softmax(_dot_nt(q, ck) * SCALE, cmask)
        o_c.append(_dot(p.astype(BF16), cv))
        psum = psum + p

    ovt = ovt_ref[...]
    imp_t = sum(_dot_nt(ovt, part) for part in _split3(psum))
    blk = _iota((nsp, 1), 0)
    score = _block_scores(imp_t, blk, t_row, n_sel)
    sc_ref[...] = score

    def rank_step(sp, rank):
        row = sc_ref[pl.ds(sp, 1), :]
        beats = (row > score) | ((row == score) & (sp < blk))
        return rank + jnp.where(beats, 1.0, 0.0)

    rank = lax.fori_loop(0, n_sel, rank_step, jnp.zeros((nsp, tq), F32))
    sel = jnp.where(rank < k_top, 1.0, 0.0).T.astype(BF16)

    qr = [qr_ref[:, h * HEAD_DIM:(h + 1) * HEAD_DIM] for h in range(HEADS_PER_GROUP)]

    def sel_step(kb, carry):
        ms, ls, accs = carry
        k0 = pl.multiple_of(kb * tk, tk)
        k = ks_ref[pl.ds(k0, tk), :].astype(BF16)
        v = vs_ref[pl.ds(k0, tk), :].astype(BF16)
        chosen = _dot_nt(sel, et_ref[pl.ds(k0, tk), :])
        mask = (chosen > 0.5) & (k0 + _iota((1, tk), 1) <= t_col)
        new_ms, new_ls, new_accs = [], [], []
        for h in range(HEADS_PER_GROUP):
            s = jnp.where(mask, _dot_nt(qr[h], k) * SCALE, NEG_BIG)
            m_new = jnp.maximum(ms[h], jnp.max(s, axis=-1, keepdims=True))
            p = jnp.where(mask, jnp.exp(s - m_new), 0.0)
            alpha = jnp.exp(ms[h] - m_new)
            new_ms.append(m_new)
            new_ls.append(alpha * ls[h] + jnp.sum(p, axis=-1, keepdims=True))
            new_accs.append(alpha * accs[h] + _dot(p.astype(BF16), v))
        return tuple(new_ms), tuple(new_ls), tuple(new_accs)

    init = (tuple(jnp.full((tq, 1), NEG_BIG, F32) for _ in range(HEADS_PER_GROUP)),
            tuple(jnp.zeros((tq, 1), F32) for _ in range(HEADS_PER_GROUP)),
            tuple(jnp.zeros((tq, HEAD_DIM), F32) for _ in range(HEADS_PER_GROUP)))
    n_kt = (t0 + tq + tk - 1) // tk
    _, ls, accs = lax.fori_loop(0, n_kt, sel_step, init)
    o_s = [accs[h] / jnp.where(ls[h] > 0.0, ls[h], 1.0) for h in range(HEADS_PER_GROUP)]

    w0 = pl.multiple_of(jnp.maximum(t0 + tq - wlen, 0), tq)
    kw = kw_ref[pl.ds(w0, wlen), :].astype(BF16)
    vw = vw_ref[pl.ds(w0, wlen), :].astype(BF16)
    dt = t_col - (w0 + _iota((1, wlen), 1))
    wmask = (dt >= 0) & (dt < WINDOW)

    gates = jax.nn.sigmoid(ga_ref[...])
    for h in range(HEADS_PER_GROUP):
        p = _masked_softmax(_dot_nt(qr[h], kw) * SCALE, wmask)
        o_w = _dot(p.astype(BF16), vw)
        g_c = gates[:, 3 * h:3 * h + 1]
        g_s = gates[:, 3 * h + 1:3 * h + 2]
        g_w = gates[:, 3 * h + 2:3 * h + 3]
        o_ref[:, h * HEAD_DIM:(h + 1) * HEAD_DIM] = g_c * o_c[h] + g_s * o_s[h] + g_w * o_w


def _sel_geometry(t_all, nc_pad, ns_pad):
    n_cmp = (t_all - CMP_BLOCK) // CMP_STRIDE + 1
    n_sel = -(-t_all // SEL_BLOCK)
    cmp_start = np.arange(n_cmp) * CMP_STRIDE
    sel_start = np.arange(n_sel) * SEL_BLOCK
    overlap = ((cmp_start[:, None] < sel_start[None, :] + SEL_BLOCK)
               & (cmp_start[:, None] + CMP_BLOCK > sel_start[None, :])).astype(np.float32)
    ovt = np.zeros((ns_pad, nc_pad), np.float32)
    ovt[:n_sel, :n_cmp] = overlap.T
    return n_cmp, n_sel, ovt


def _nsa_prompt(qn, qr, z, ckv, nsa_rows, win_rows, b, t, off_ga):
    tq = 128
    tk = min(512, t)
    nq = t // tq
    nc = t // CMP_STRIDE
    ns_pad = LANES
    n_cmp, n_sel, ovt = _sel_geometry(t, nc, ns_pad)
    assert n_sel <= ns_pad
    k_top = min(N_SELECT, n_sel)
    wlen = min(WINDOW + tq, t)
    et = (np.arange(t)[:, None] // SEL_BLOCK == np.arange(ns_pad)[None, :]).astype(np.float32)
    gab = off_ga // LANES
    kernel = functools.partial(_nsa_body, tq=tq, tk=tk, t_all=t, n_cmp=n_cmp, n_sel=n_sel,
                               k_top=k_top, wlen=wlen)
    gw = HEADS_PER_GROUP * HEAD_DIM
    return pl.pallas_call(
        kernel,
        grid=(b, NSA_KV_GROUPS, nq),
        in_specs=[pl.BlockSpec((tq, gw), lambda i, g, j: (i * nq + j, g)),
                  pl.BlockSpec((tq, gw), lambda i, g, j: (i * nq + j, g)),
                  pl.BlockSpec((tq, LANES), lambda i, g, j: (i * nq + j, gab + g)),
                  pl.BlockSpec((None, None, None, nc, HEAD_DIM), lambda i, g, j: (i, g, 0, 0, 0)),
                  pl.BlockSpec((None, None, None, nc, HEAD_DIM), lambda i, g, j: (i, g, 1, 0, 0)),
                  pl.BlockSpec((t, HEAD_DIM), lambda i, g, j: (i, 2 * NSA_KV_GROUPS + g)),
                  pl.BlockSpec((t, HEAD_DIM), lambda i, g, j: (i, 3 * NSA_KV_GROUPS + g)),
                  pl.BlockSpec((t, HEAD_DIM), lambda i, g, j: (i, g)),
                  pl.BlockSpec((t, HEAD_DIM), lambda i, g, j: (i, NSA_KV_GROUPS + g)),
                  pl.BlockSpec((ns_pad, nc), lambda i, g, j: (0, 0)),
                  pl.BlockSpec((t, ns_pad), lambda i, g, j: (0, 0))],
        out_specs=pl.BlockSpec((tq, gw), lambda i, g, j: (i * nq + j, g)),
        out_shape=jax.ShapeDtypeStruct((b * t, NSA_W), F32),
        scratch_shapes=[pltpu.VMEM((ns_pad, tq), F32)],
        compiler_params=_params(("parallel", "parallel", "arbitrary"), 56),
        name="nsa_prompt",
    )(qn, qr, z, ckv, ckv, nsa_rows, nsa_rows, win_rows, win_rows,
      jnp.asarray(ovt, BF16), jnp.asarray(et, BF16))


def _sb_body(q_ref, k_ref, v_ref, u_ref, o_ref, *, tq):
    j = pl.program_id(2)
    t_col = j * tq + _iota((tq, 1), 0)
    q = (q_ref[...] * SCALE).astype(BF16)
    u = u_ref[...]

    def step(i, carry):
        c, acc = carry
        k0 = pl.multiple_of((j - i) * tq, tq)
        k = k_ref[pl.ds(k0, tq), :].astype(BF16)
        v = v_ref[pl.ds(k0, tq), :].astype(BF16)
        z = _dot_nt(q, k)
        mask = k0 + _iota((1, tq), 1) < t_col
        sp = _softplus(z)
        log1m = jnp.where(mask, -sp, 0.0)
        hi = log1m.astype(BF16)
        lo = (log1m - hi.astype(F32)).astype(BF16)
        suffix = _dot(hi, u) + _dot(lo, u)
        a = jnp.where(mask, jnp.exp(z - sp + c + suffix - log1m), 0.0)
        return c + suffix[:, 0:1], acc + _dot(a.astype(BF16), v)

    _, acc = lax.fori_loop(0, j + 1, step, (jnp.zeros((tq, 1), F32), jnp.zeros((tq, HEAD_DIM), F32)))
    o_ref[...] = acc


def _sb_prompt(z, b, t):
    tq = 128
    nq = t // tq
    u = (np.arange(tq)[:, None] >= np.arange(tq)[None, :]).astype(np.float32)
    qb, kb, vb = OFF_QB // HEAD_DIM, OFF_KB // HEAD_DIM, OFF_VB // HEAD_DIM
    return pl.pallas_call(
        functools.partial(_sb_body, tq=tq),
        grid=(b, SB_HEADS, nq),
        in_specs=[pl.BlockSpec((tq, HEAD_DIM), lambda i, h, j: (i * nq + j, qb + h)),
                  pl.BlockSpec((t, HEAD_DIM), lambda i, h, j: (i, kb + h)),
                  pl.BlockSpec((t, HEAD_DIM), lambda i, h, j: (i, vb + h)),
                  pl.BlockSpec((tq, tq), lambda i, h, j: (0, 0))],
        out_specs=pl.BlockSpec((tq, HEAD_DIM), lambda i, h, j: (i * nq + j, h)),
        out_shape=jax.ShapeDtypeStruct((b * t, SB_W), F32),
        compiler_params=_params(("parallel", "parallel", "arbitrary"), 32),
        name="sb_prompt",
    )(z, z, z, jnp.asarray(u, BF16))


def _mix_body(oa_ref, ob_ref, za_ref, zb_ref, m0_ref, m1_ref, x_ref, wa_ref, wb_ref, wo_ref, h_ref):
    ua = (oa_ref[...] * _silu(za_ref[...])).astype(BF16)
    ub = (ob_ref[...] * _silu(zb_ref[...])).astype(BF16)
    ya = _dot(ua, wa_ref[...])
    yb = _dot(ub, wb_ref[...])
    mix = jax.nn.sigmoid(m0_ref[...]) * ya + jax.nn.sigmoid(m1_ref[...]) * yb
    h_ref[...] = x_ref[...] + _dot(mix.astype(BF16), wo_ref[...])


def _mix(o_a, o_b, z, x, w_up_a, w_up_b, w_out, l):
    n, d = x.shape
    tm = min(n, 256)
    once = pl.Buffered(1)
    return pl.pallas_call(
        _mix_body,
        grid=(n // tm,),
        in_specs=[pl.BlockSpec((tm, NSA_W), lambda i: (i, 0)),
                  pl.BlockSpec((tm, SB_W), lambda i: (i, 0)),
                  pl.BlockSpec((tm, NSA_W), lambda i: (i, OFF_ZA // NSA_W)),
                  pl.BlockSpec((tm, SB_W), lambda i: (i, OFF_ZB // SB_W)),
                  pl.BlockSpec((tm, d), lambda i: (i, OFF_M0 // d)),
                  pl.BlockSpec((tm, d), lambda i: (i, OFF_M0 // d + 1)),
                  pl.BlockSpec((tm, d), lambda i: (i, 0)),
                  pl.BlockSpec((None, NSA_W, d), lambda i: (l, 0, 0), pipeline_mode=once),
                  pl.BlockSpec((None, SB_W, d), lambda i: (l, 0, 0), pipeline_mode=once),
                  pl.BlockSpec((None, d, d), lambda i: (l, 0, 0), pipeline_mode=once)],
        out_specs=pl.BlockSpec((tm, d), lambda i: (i, 0)),
        out_shape=jax.ShapeDtypeStruct((n, d), F32),
        compiler_params=_params(("parallel",), 56),
        name="mix",
    )(o_a, o_b, z, z, z, z, x, w_up_a, w_up_b, w_out)


def _ple_body(h_ref, p_ref, g_ref, wpg_ref, wple_ref, o_ref):
    h = h_ref[...]
    hn = (h * lax.rsqrt(jnp.mean(h * h, axis=-1, keepdims=True) + RMS_EPS) * g_ref[...]).astype(BF16)
    gate = jax.nn.sigmoid(_dot(hn, wpg_ref[...]))
    o_ref[...] = h + gate * _dot(p_ref[...].astype(BF16), wple_ref[...])


def _ple(h, p, g, w_pg, w_ple, l):
    n, d = h.shape
    pd = p.shape[1]
    tm = min(n, 256)
    once = pl.Buffered(1)
    return pl.pallas_call(
        _ple_body,
        grid=(n // tm,),
        in_specs=[pl.BlockSpec((tm, d), lambda i: (i, 0)),
                  pl.BlockSpec((tm, pd), lambda i: (i, 0)),
                  pl.BlockSpec((1, d), lambda i: (0, 0)),
                  pl.BlockSpec((None, d, d), lambda i: (l, 0, 0), pipeline_mode=once),
                  pl.BlockSpec((None, pd, d), lambda i: (l, 0, 0), pipeline_mode=once)],
        out_specs=pl.BlockSpec((tm, d), lambda i: (i, 0)),
        out_shape=jax.ShapeDtypeStruct((n, d), F32),
        compiler_params=_params(("parallel",), 40),
        name="ple",
    )(h, p, g, w_pg, w_ple)


def _sbdec_body(pt_ref, qm_ref, page_ref, tri_ref, ex_ref, o_ref, c_ref, acc_ref):
    del pt_ref
    p = pl.program_id(1)

    @pl.when(p == 0)
    def _():
        c_ref[...] = jnp.zeros_like(c_ref)
        acc_ref[...] = jnp.zeros_like(acc_ref)

    k = page_ref[:, 0:SB_W].astype(BF16)
    v = page_ref[:, SB_W:2 * SB_W]
    z = _dot(k, qm_ref[...])
    sp = _softplus(z)
    log1m = -sp
    hi = log1m.astype(BF16)
    lo = (log1m - hi.astype(F32)).astype(BF16)
    tri = tri_ref[...]
    suffix = _dot(tri, hi) + _dot(tri, lo)
    a = jnp.exp(z - sp + c_ref[...] + suffix - log1m)
    a_wide = _dot(a.astype(BF16), ex_ref[...])
    acc_ref[...] += jnp.sum(a_wide * v, axis=0, keepdims=True)
    c_ref[...] += suffix[0:1, :]

    @pl.when(p == pl.num_programs(1) - 1)
    def _():
        o_ref[...] = acc_ref[...]


def _sb_sample(q_b, cache_sb, page_table, l):
    b, n_pages = page_table.shape
    head_of_row = np.arange(SB_W) // HEAD_DIM
    onehot = (head_of_row[:, None] == np.arange(LANES)[None, :])
    qm = (jnp.where(jnp.asarray(onehot)[None], (q_b * SCALE)[:, :, None], 0.0)).astype(BF16)
    tri = (np.arange(PAGE_SIZE)[:, None] <= np.arange(PAGE_SIZE)[None, :]).astype(np.float32)
    grid_spec = pltpu.PrefetchScalarGridSpec(
        num_scalar_prefetch=1,
        grid=(b, n_pages),
        in_specs=[pl.BlockSpec((None, SB_W, LANES), lambda i, p, pt: (i, 0, 0)),
                  pl.BlockSpec((None, None, PAGE_SIZE, 2 * SB_W),
                               lambda i, p, pt: (l, pt[i * n_pages + n_pages - 1 - p], 0, 0)),
                  pl.BlockSpec((PAGE_SIZE, PAGE_SIZE), lambda i, p, pt: (0, 0)),
                  pl.BlockSpec((LANES, SB_W), lambda i, p, pt: (0, 0))],
        out_specs=pl.BlockSpec((None, 1, SB_W), lambda i, p, pt: (i, 0, 0)),
        scratch_shapes=[pltpu.VMEM((1, LANES), F32), pltpu.VMEM((1, SB_W), F32)])
    out = pl.pallas_call(
        _sbdec_body,
        grid_spec=grid_spec,
        out_shape=jax.ShapeDtypeStruct((b, 1, SB_W), F32),
        compiler_params=_params(("parallel", "arbitrary"), 32),
        name="sb_sample",
    )(page_table.reshape(-1), qm, cache_sb, jnp.asarray(tri, BF16), jnp.asarray(onehot.T, BF16))
    return out.reshape(b, SB_W)


def _nsadec_cmp_body(*refs, n_pc, n_cmp, n_sel, k_top, t_q):
    pt_ref, qn_ref = refs[0], refs[1]
    page_refs = refs[2:2 + n_pc]
    pos_ref, w1_ref, b1_ref, w2_ref, ov_ref, sel_ref, oc_ref, p_ref, q_ref, stage_ref = refs[2 + n_pc:]
    del pt_ref
    c = pl.program_id(1)
    rows_pp = PAGE_SIZE // CMP_STRIDE
    half = CMP_STRIDE * HEAD_DIM
    n_chunk = n_pc * rows_pp

    for i in range(n_pc):
        for kg in range(2 * NSA_KV_GROUPS):
            stage_ref[kg, i * PAGE_SIZE:(i + 1) * PAGE_SIZE, :] = page_refs[i][:, kg * HEAD_DIM:(kg + 1) * HEAD_DIM]

    for kind in range(2):
        r = jnp.concatenate(
            [_cmp_rows(stage_ref.at[kind * NSA_KV_GROUPS + g], n_chunk, 0) for g in range(NSA_KV_GROUPS)],
            axis=0)
        p = _dot((r + pos_ref[kind, 0:1, :]).astype(BF16), w1_ref[kind, 0:half, :])
        q = _dot((r + pos_ref[kind, 1:2, :]).astype(BF16), w1_ref[kind, half:2 * half, :])
        r0 = pl.multiple_of(c * n_chunk, n_chunk)
        for g in range(NSA_KV_GROUPS):
            p_ref[kind, g, pl.ds(r0, n_chunk), :] = p[g * n_chunk:(g + 1) * n_chunk]
            q_ref[kind, g, pl.ds(r0, n_chunk), :] = q[g * n_chunk:(g + 1) * n_chunk]

    @pl.when(c == pl.num_programs(1) - 1)
    def _():
        nc = p_ref.shape[2]
        nsp = ov_ref.shape[1]
        qn = qn_ref[...]
        head_group = _iota((NSA_HEADS, 1), 0) // HEADS_PER_GROUP
        n_idx = _iota((1, nc), 1)
        cmask = (n_idx * CMP_STRIDE + (CMP_BLOCK - 1) <= t_q) & (n_idx < n_cmp)
        blk = _iota((1, nsp), 1)
        o_c = jnp.zeros((NSA_HEADS, HEAD_DIM), F32)
        sel_rows = jnp.zeros((NSA_HEADS, nsp), F32)
        for g in range(NSA_KV_GROUPS):
            ckv = []
            for kind in range(2):
                hid = _cmp_hidden(p_ref[kind, g], q_ref[kind, g], b1_ref[kind])
                ckv.append(_dot(hid.astype(BF16), w2_ref[kind]).astype(BF16))
            p = _masked_softmax(_dot_nt(qn, ckv[0]) * SCALE, cmask)
            mine = head_group == g
            o_c = jnp.where(mine, _dot(p.astype(BF16), ckv[1]), o_c)
            psum = jnp.sum(jnp.where(mine, p, 0.0), axis=0, keepdims=True)
            psum = jnp.broadcast_to(psum, (8, nc))
            imp = sum(_dot(part, ov_ref[...]) for part in _split3(psum))[0:1, :]
            score = _block_scores(imp, blk, t_q, n_sel)
            along_lanes = jnp.broadcast_to(score, (nsp, nsp))
            along_rows = along_lanes.T
            beats = (along_rows > along_lanes) | ((along_rows == along_lanes)
                                                  & (_iota((nsp, nsp), 0) < _iota((nsp, nsp), 1)))
            rank = jnp.sum(jnp.where(beats, 1.0, 0.0), axis=0, keepdims=True)
            chosen = jnp.where(rank < k_top, 1.0, 0.0)
            sel_rows = jnp.where(_iota((NSA_HEADS, 1), 0) == g, chosen, sel_rows)
        sel_ref[...] = sel_rows
        oc_ref[...] = o_c


def _nsa_sample_select(qn, cache_nsa, page_table, pos, w1, b1, w2, l, past_len):
    b, n_pages = page_table.shape
    n_pc = min(8, n_pages)
    t_all = past_len + 1
    nc = past_len // CMP_STRIDE
    ns_pad = -(-(-(-t_all // SEL_BLOCK)) // LANES) * LANES
    n_cmp, n_sel, ovt = _sel_geometry(t_all, nc, ns_pad)
    k_top = min(N_SELECT, n_sel)
    half = CMP_STRIDE * HEAD_DIM
    kernel = functools.partial(_nsadec_cmp_body, n_pc=n_pc, n_cmp=n_cmp, n_sel=n_sel, k_top=k_top,
                               t_q=past_len)

    def page_spec(i_pc):
        return pl.BlockSpec((None, None, PAGE_SIZE, 2 * KV_W),
                            lambda i, c, pt: (l, pt[i * n_pages + c * n_pc + i_pc], 0, 0))

    grid_spec = pltpu.PrefetchScalarGridSpec(
        num_scalar_prefetch=1,
        grid=(b, n_pages // n_pc),
        in_specs=[pl.BlockSpec((None, NSA_HEADS, HEAD_DIM), lambda i, c, pt: (i, 0, 0))]
        + [page_spec(i_pc) for i_pc in range(n_pc)]
        + [pl.BlockSpec((None, 2, 2, half), lambda i, c, pt: (l, 0, 0, 0)),
           pl.BlockSpec((None, 2, 2 * half, HEAD_DIM), lambda i, c, pt: (l, 0, 0, 0)),
           pl.BlockSpec((None, 2, 1, HEAD_DIM), lambda i, c, pt: (l, 0, 0, 0)),
           pl.BlockSpec((None, 2, HEAD_DIM, HEAD_DIM), lambda i, c, pt: (l, 0, 0, 0)),
           pl.BlockSpec((nc, ns_pad), lambda i, c, pt: (0, 0))],
        out_specs=[pl.BlockSpec((None, NSA_HEADS, ns_pad), lambda i, c, pt: (i, 0, 0)),
                   pl.BlockSpec((None, NSA_HEADS, HEAD_DIM), lambda i, c, pt: (i, 0, 0))],
        scratch_shapes=[pltpu.VMEM((2, NSA_KV_GROUPS, nc, HEAD_DIM), F32),
                        pltpu.VMEM((2, NSA_KV_GROUPS, nc, HEAD_DIM), F32),
                        pltpu.VMEM((2 * NSA_KV_GROUPS, n_pc * PAGE_SIZE, HEAD_DIM), F32)])
    sel, o_c = pl.pallas_call(
        kernel,
        grid_spec=grid_spec,
        out_shape=[jax.ShapeDtypeStruct((b, NSA_HEADS, ns_pad), F32),
                   jax.ShapeDtypeStruct((b, NSA_HEADS, HEAD_DIM), F32)],
        compiler_params=_params(("parallel", "arbitrary"), 48),
        name="nsa_sample_select",
    )(page_table.reshape(-1), qn, *([cache_nsa] * n_pc), pos, w1, b1, w2,
      jnp.asarray(ovt.T, BF16))
    return sel, o_c, n_sel, k_top


def _nsadec_att_body(idx_ref, pt_ref, qr_ref, gate_ref, oc_ref, ks_ref, vs_ref, ksn_ref, vsn_ref,
                     kwn_ref, vwn_ref, kw_ref, vw_ref, o_ref, m_ref, l_ref, acc_ref, *, k_top, n_sel):
    del pt_ref
    i = pl.program_id(0)
    g = pl.program_id(1)
    s = pl.program_id(2)
    q = qr_ref[...]
    base = (i * NSA_KV_GROUPS + g) * k_top

    @pl.when(s == 0)
    def _():
        m_ref[...] = jnp.full_like(m_ref, NEG_BIG)
        l_ref[...] = jnp.zeros_like(l_ref)
        acc_ref[...] = jnp.zeros_like(acc_ref)

    def update(sc, v):
        m_new = jnp.maximum(m_ref[...], jnp.max(sc, axis=-1, keepdims=True))
        p = jnp.exp(sc - m_new)
        alpha = jnp.exp(m_ref[...] - m_new)
        l_ref[...] = alpha * l_ref[...] + jnp.sum(p, axis=-1, keepdims=True)
        acc_ref[...] = alpha * acc_ref[...] + _dot(p.astype(BF16), v.astype(BF16))
        m_ref[...] = m_new

    @pl.when(idx_ref[base + s] < n_sel - 1)
    def _():
        update(_dot_nt(q, ks_ref[...].astype(BF16)) * SCALE, vs_ref[...])

    @pl.when(s == k_top - 1)
    def _():
        qf = q.astype(F32)
        n_new = lax.fori_loop(
            0, k_top, lambda r, a: a + (idx_ref[base + r] == n_sel - 1).astype(jnp.int32), jnp.int32(0))
        has_new = n_new > 0
        s_new = jnp.sum(qf * ksn_ref[...], axis=-1, keepdims=True) * SCALE
        s_new = jnp.where(has_new, s_new, NEG_BIG)
        m_new = jnp.maximum(m_ref[...], s_new)
        p_new = jnp.where(has_new, jnp.exp(s_new - m_new), 0.0)
        alpha = jnp.exp(m_ref[...] - m_new)
        l_s = alpha * l_ref[...] + p_new
        o_s = (alpha * acc_ref[...] + p_new * vsn_ref[...]) / jnp.where(l_s > 0.0, l_s, 1.0)

        w_buf = kw_ref.shape[0]
        sw = _dot_nt(q, kw_ref[...].astype(BF16)) * SCALE
        wmask = (w_buf - _iota((1, w_buf), 1)) < WINDOW
        sw = jnp.where(wmask, sw, NEG_BIG)
        sw_new = jnp.sum(qf * kwn_ref[...], axis=-1, keepdims=True) * SCALE
        mw = jnp.maximum(jnp.max(sw, axis=-1, keepdims=True), sw_new)
        pw = jnp.where(wmask, jnp.exp(sw - mw), 0.0)
        pw_new = jnp.exp(sw_new - mw)
        l_w = jnp.sum(pw, axis=-1, keepdims=True) + pw_new
        o_w = (_dot(pw.astype(BF16), vw_ref[...].astype(BF16)) + pw_new * vwn_ref[...]) / l_w

        gates = jax.nn.sigmoid(gate_ref[...])
        o_ref[...] = gates[0] * oc_ref[...] + gates[1] * o_s + gates[2] * o_w


def _nsa_sample_attend(idx, qr, gates, o_c, cache_nsa, page_table, nsa_new, win_new, state_win, l,
                       n_sel, k_top):
    b, n_pages = page_table.shape
    w_buf = state_win.shape[2]
    per_page = PAGE_SIZE // SEL_BLOCK
    last = n_sel - 2
    g_n = NSA_KV_GROUPS

    def blk_of(i, g, s, idx_ref):
        return jnp.minimum(idx_ref[(i * g_n + g) * k_top + s], last)

    def ks_map(i, g, s, idx_ref, pt):
        blk = blk_of(i, g, s, idx_ref)
        return (l, pt[i * n_pages + blk // per_page], blk % per_page, 0, 2 * g_n + g)

    def vs_map(i, g, s, idx_ref, pt):
        blk = blk_of(i, g, s, idx_ref)
        return (l, pt[i * n_pages + blk // per_page], blk % per_page, 0, 3 * g_n + g)

    grid_spec = pltpu.PrefetchScalarGridSpec(
        num_scalar_prefetch=2,
        grid=(b, g_n, k_top),
        in_specs=[pl.BlockSpec((None, NSA_HEADS, HEAD_DIM), lambda i, g, s, *_: (i, 0, 0)),
                  pl.BlockSpec((None, 3, NSA_HEADS, HEAD_DIM), lambda i, g, s, *_: (i, 0, 0, 0)),
                  pl.BlockSpec((None, NSA_HEADS, HEAD_DIM), lambda i, g, s, *_: (i, 0, 0)),
                  pl.BlockSpec((None, None, None, SEL_BLOCK, HEAD_DIM), ks_map),
                  pl.BlockSpec((None, None, None, SEL_BLOCK, HEAD_DIM), vs_map),
                  pl.BlockSpec((None, 1, HEAD_DIM), lambda i, g, s, *_: (i, 0, 2 * g_n + g)),
                  pl.BlockSpec((None, 1, HEAD_DIM), lambda i, g, s, *_: (i, 0, 3 * g_n + g)),
                  pl.BlockSpec((None, 1, HEAD_DIM), lambda i, g, s, *_: (i, 0, g)),
                  pl.BlockSpec((None, 1, HEAD_DIM), lambda i, g, s, *_: (i, 0, g_n + g)),
                  pl.BlockSpec((None, None, w_buf, HEAD_DIM), lambda i, g, s, *_: (l, i, 0, g)),
                  pl.BlockSpec((None, None, w_buf, HEAD_DIM), lambda i, g, s, *_: (l, i, 0, g_n + g))],
        out_specs=pl.BlockSpec((None, None, NSA_HEADS, HEAD_DIM), lambda i, g, s, *_: (i, g, 0, 0)),
        scratch_shapes=[pltpu.VMEM((NSA_HEADS, 1), F32), pltpu.VMEM((NSA_HEADS, 1), F32),
                        pltpu.VMEM((NSA_HEADS, HEAD_DIM), F32)])
    return pl.pallas_call(
        functools.partial(_nsadec_att_body, k_top=k_top, n_sel=n_sel),
        grid_spec=grid_spec,
        out_shape=jax.ShapeDtypeStruct((b, g_n, NSA_HEADS, HEAD_DIM), F32),
        compiler_params=_params(("parallel", "parallel", "arbitrary"), 32),
        name="nsa_sample_attend",
    )(idx.reshape(-1), page_table.reshape(-1), qr, gates, o_c, cache_nsa, cache_nsa,
      nsa_new, nsa_new, win_new, win_new, state_win, state_win)


def _permute_w_in(w_in, d_model):
    o = np.cumsum([0, NSA_W, 6 * KV_W, 3 * NSA_HEADS, NSA_W, 3 * SB_W, SB_W, 2 * d_model])
    depth = w_in.shape[0]
    gate = w_in[:, :, o[2]:o[3]]
    per_group = 3 * HEADS_PER_GROUP
    gate_cols = []
    for g in range(NSA_KV_GROUPS):
        gate_cols.append(gate[:, :, g * per_group:(g + 1) * per_group])
        gate_cols.append(jnp.zeros((depth, d_model, LANES - per_group), w_in.dtype))
    gate_cols.append(jnp.zeros((depth, d_model, GA_W - NSA_KV_GROUPS * LANES), w_in.dtype))
    cols = [w_in[:, :, o[0]:o[1]], w_in[:, :, o[3]:o[4]], w_in[:, :, o[5]:o[6]], w_in[:, :, o[4]:o[5]],
            w_in[:, :, o[6]:o[7]], w_in[:, :, o[1]:o[2]]] + gate_cols
    return jnp.concatenate(cols, axis=-1).astype(BF16)


def kernel(x_prompt, x_sample, cache_nsa, cache_sb, state_win, page_table, p_prompt, p_sample, g_mix,
           w_in, g_q, g_k, cmp_pos, w_c1, b_c1, w_c2, w_up_a, w_up_b, w_out, g_ple, w_pg, w_ple):
    depth = w_in.shape[0]
    bp, t, d = x_prompt.shape
    bs, ts, _ = x_sample.shape
    assert ts == 1
    n_pages = page_table.shape[1]
    past_len = n_pages * PAGE_SIZE
    n_pool = cache_nsa.shape[1]
    w_buf = state_win.shape[2]
    off_m1, off_kv, off_ga, n_cols = _offsets(d)
    half = CMP_STRIDE * HEAD_DIM

    w_proj = _permute_w_in(w_in, d)
    wa, wb, wo = w_up_a.astype(BF16), w_up_b.astype(BF16), w_out.astype(BF16)
    wpg, wple = w_pg.astype(BF16), w_ple.astype(BF16)
    pos2 = cmp_pos.reshape(depth, 2, 2, half)
    w1 = w_c1.reshape(depth, 2, 2 * half, HEAD_DIM).astype(BF16)
    b1 = b_c1.reshape(depth, 2, 1, HEAD_DIM)
    w2 = w_c2.astype(BF16)
    cos_p, sin_p = _rope_tables(jnp.arange(t))
    cos_s, sin_s = _rope_tables(jnp.full((bs,), past_len))
    nsa_pages = cache_nsa.reshape(depth, n_pool, PAGE_SIZE, 4 * KV_W)
    nsa_blocks = cache_nsa.reshape(depth, n_pool, PAGE_SIZE // SEL_BLOCK, SEL_BLOCK, 4 * KV_W)
    sb_pages = cache_sb.reshape(depth, n_pool, PAGE_SIZE, 2 * SB_W)
    win_state = state_win.reshape(depth, bs, w_buf, 2 * KV_W)

    hp = x_prompt.reshape(bp * t, d)
    hs = x_sample.reshape(bs, d)
    outs = [[] for _ in range(6)]
    for l in range(depth):
        gm, gq, gp = g_mix[l][None], g_q[l][None], g_ple[l][None]

        z = _proj(hp, gm, w_proj, l)
        qn, qr, nsa_rows, win_rows = _prep(z, cos_p, sin_p, gq, g_k[l], off_kv)
        ckv = _compress(nsa_rows, bp, t, pos2, w1, b1, w2, l)
        o_a = _nsa_prompt(qn, qr, z, ckv, nsa_rows, win_rows, bp, t, off_ga)
        o_b = _sb_prompt(z, bp, t)
        h1 = _mix(o_a, o_b, z, hp, wa, wb, wo, l)
        hp = _ple(h1, p_prompt[l].reshape(bp * t, -1), gp, wpg, wple, l)
        outs[0].append(nsa_rows.reshape(bp, t, 4, NSA_KV_GROUPS, HEAD_DIM))
        outs[2].append(z[:, OFF_KB:OFF_KB + 2 * SB_W].reshape(bp, t, 2, SB_HEADS, HEAD_DIM))
        keep = min(WINDOW, t)
        outs[4].append(win_rows.reshape(bp, t, 2, NSA_KV_GROUPS, HEAD_DIM)[:, t - keep:])

        zs = _proj(hs, gm, w_proj, l)
        qn_s, qr_s, nsa_new, win_new = _prep(zs, cos_s, sin_s, gq, g_k[l], off_kv)
        sel, o_c, n_sel, k_top = _nsa_sample_select(
            qn_s.reshape(bs, NSA_HEADS, HEAD_DIM), nsa_pages, page_table, pos2, w1, b1, w2, l, past_len)
        _, idx = lax.top_k(sel[:, :NSA_KV_GROUPS, :], k_top)
        gate_logits = zs[:, off_ga:off_ga + NSA_KV_GROUPS * LANES].reshape(bs, NSA_KV_GROUPS, LANES)
        gate_logits = gate_logits[:, :, :3 * HEADS_PER_GROUP].reshape(bs, NSA_HEADS, 3)
        gate_logits = jnp.broadcast_to(gate_logits.transpose(0, 2, 1)[..., None], (bs, 3, NSA_HEADS, HEAD_DIM))
        o_full = _nsa_sample_attend(idx.astype(jnp.int32), qr_s.reshape(bs, NSA_HEADS, HEAD_DIM), gate_logits,
                                    o_c, nsa_blocks, page_table, nsa_new.reshape(bs, 1, -1),
                                    win_new.reshape(bs, 1, -1), win_state, l, n_sel, k_top)
        o_a_s = jnp.concatenate(
            [o_full[:, g, g * HEADS_PER_GROUP:(g + 1) * HEADS_PER_GROUP] for g in range(NSA_KV_GROUPS)],
            axis=1).reshape(bs, NSA_W)
        o_b_s = _sb_sample(zs[:, OFF_QB:OFF_QB + SB_W], sb_pages, page_table, l)
        h1s = _mix(o_a_s, o_b_s, zs, hs, wa, wb, wo, l)
        hs = _ple(h1s, p_sample[l].reshape(bs, -1), gp, wpg, wple, l)
        outs[1].append(nsa_new.reshape(bs, 1, 4, NSA_KV_GROUPS, HEAD_DIM))
        outs[3].append(zs[:, OFF_KB:OFF_KB + 2 * SB_W].reshape(bs, 1, 2, SB_HEADS, HEAD_DIM))
        new_win = jnp.concatenate([win_state[l], win_new.reshape(bs, 1, -1)], axis=1)[:, 1:]
        outs[5].append(new_win.reshape(bs, w_buf, 2, NSA_KV_GROUPS, HEAD_DIM))

    return (hp.reshape(bp, t, d), hs.reshape(bs, 1, d), jnp.stack(outs[0]), jnp.stack(outs[1]),
            jnp.stack(outs[2]), jnp.stack(outs[3]), jnp.stack(outs[4]), jnp.stack(outs[5]))
```

```python
import functools

import numpy as np
import jax
import jax.numpy as jnp
from jax import lax
from jax.experimental import pallas as pl
from jax.experimental.pallas import tpu as pltpu

HEAD_DIM = 128
NSA_HEADS = 8
NSA_KV_GROUPS = 2
HEADS_PER_GROUP = NSA_HEADS // NSA_KV_GROUPS
SB_HEADS = 8
CMP_BLOCK = 32
CMP_STRIDE = 16
SEL_BLOCK = 64
N_SELECT = 16
WINDOW = 512
ROPE_THETA = 10000.0
PAGE_SIZE = 128
RMS_EPS = 1e-6
SEL_BONUS = 1e3
NEG_BIG = -1e30
BELOW_NEG_BIG = -3e38
SB_UNDERFLOW = 104.0

NSA_W = NSA_HEADS * HEAD_DIM
KV_W = NSA_KV_GROUPS * HEAD_DIM
SB_W = SB_HEADS * HEAD_DIM
NSA_ROWS = 4 * NSA_KV_GROUPS
WIN_ROWS = 2 * NSA_KV_GROUPS
SB_ROWS = 2 * SB_HEADS
LANES = 128
SUBLANES = 8
SCALE = HEAD_DIM ** -0.5
V7X_VMEM_BYTES = 64 * 1024 * 1024

F32 = jnp.float32
BF16 = jnp.bfloat16

OFF_QA = 0
OFF_ZA = NSA_W
OFF_ZB = OFF_ZA + NSA_W
OFF_QB = OFF_ZB + SB_W
OFF_KB = OFF_QB + SB_W
OFF_VB = OFF_KB + SB_W
OFF_M0 = OFF_VB + SB_W
GA_W = 512


def _offsets(d_model):
    off_m1 = OFF_M0 + d_model
    off_kv = off_m1 + d_model
    off_ga = off_kv + 6 * KV_W
    return off_m1, off_kv, off_ga, off_ga + GA_W


def _params(sem, vmem_mb):
    return pltpu.CompilerParams(dimension_semantics=sem,
                                vmem_limit_bytes=min(vmem_mb * 1024 * 1024, V7X_VMEM_BYTES - (4 << 20)))


def _dot(a, b):
    return jnp.dot(a, b, preferred_element_type=F32)


def _dot_nt(a, b):
    return lax.dot_general(a, b, (((1,), (1,)), ((), ())), preferred_element_type=F32)


def _split2(a):
    hi = a.astype(BF16)
    return hi, (a - hi.astype(F32)).astype(BF16)


def _split3(a):
    hi = a.astype(BF16)
    r1 = a - hi.astype(F32)
    mid = r1.astype(BF16)
    lo = (r1 - mid.astype(F32)).astype(BF16)
    return hi, mid, lo


def _iota(shape, dim):
    return lax.broadcasted_iota(jnp.int32, shape, dim)


def _silu(x):
    return x * jax.nn.sigmoid(x)


def _softplus(z):
    return jnp.maximum(z, 0.0) + jnp.log1p(jnp.exp(-jnp.abs(z)))


def _masked_softmax(s, mask):
    s = jnp.where(mask, s, NEG_BIG)
    m = jnp.max(s, axis=-1, keepdims=True)
    p = jnp.where(mask, jnp.exp(s - m), 0.0)
    l = jnp.sum(p, axis=-1, keepdims=True)
    return p / jnp.where(l > 0.0, l, 1.0)


def _rows(ref, start, count, stride):
    return ref[pl.ds(start, count, stride=stride), :]


def _proj_body(x_ref, g_ref, w_ref, o_ref, xn_ref):
    @pl.when(pl.program_id(1) == 0)
    def _():
        x = x_ref[...]
        ms = jnp.mean(x * x, axis=-1, keepdims=True)
        xn_ref[...] = (x * lax.rsqrt(ms + RMS_EPS) * g_ref[...]).astype(BF16)

    o_ref[...] = _dot(xn_ref[...], w_ref[...])


def _proj(x, g, w, l):
    n, d = x.shape
    c = w.shape[2]
    tm = min(n, 1024)
    tn = 512
    return pl.pallas_call(
        _proj_body,
        grid=(n // tm, c // tn),
        in_specs=[pl.BlockSpec((tm, d), lambda i, j: (i, 0)),
                  pl.BlockSpec((1, d), lambda i, j: (0, 0)),
                  pl.BlockSpec((None, d, tn), lambda i, j: (l, 0, j))],
        out_specs=pl.BlockSpec((tm, tn), lambda i, j: (i, j)),
        out_shape=jax.ShapeDtypeStruct((n, c), F32),
        scratch_shapes=[pltpu.VMEM((tm, d), BF16)],
        compiler_params=_params(("parallel", "arbitrary"), 48),
        name="proj",
    )(x, g, w)


def _prep_body(qa_ref, kc_ref, ks_ref, kw_ref, cos_ref, sin_ref, gq_ref, gk_ref,
               qn_ref, qr_ref, kv_ref, cmp_ref, nsa_ref, win_ref, *, tm):
    cos = cos_ref[...]
    sin = sin_ref[...]

    def norm(v, g):
        return v * lax.rsqrt(jnp.mean(v * v, axis=-1, keepdims=True) + RMS_EPS) * g

    def rope(v):
        return v * cos + pltpu.roll(v, HEAD_DIM // 2, 1) * sin

    gq = gq_ref[...]
    for h in range(NSA_HEADS):
        sl = slice(h * HEAD_DIM, (h + 1) * HEAD_DIM)
        qn = norm(qa_ref[:, sl], gq)
        qn_ref[:, sl] = (qn * SCALE).astype(BF16)
        qr_ref[:, sl] = (rope(qn) * SCALE).astype(BF16)
    for g in range(NSA_KV_GROUPS):
        k_sl = slice(g * HEAD_DIM, (g + 1) * HEAD_DIM)
        v_sl = slice(KV_W + g * HEAD_DIM, KV_W + (g + 1) * HEAD_DIM)
        rows = [norm(kc_ref[:, k_sl], gk_ref[0:1, :]), kc_ref[:, v_sl],
                rope(norm(ks_ref[:, k_sl], gk_ref[1:2, :])), ks_ref[:, v_sl],
                rope(norm(kw_ref[:, k_sl], gk_ref[2:3, :])), kw_ref[:, v_sl]]
        for kind, val in enumerate(rows):
            kv_ref[:, kind * KV_W + g * HEAD_DIM:kind * KV_W + (g + 1) * HEAD_DIM] = val.astype(BF16)
        cmp_ref[:, k_sl] = rows[0]
        cmp_ref[:, v_sl] = rows[1]
        for kind in range(4):
            nsa_ref[pl.ds(kind * NSA_KV_GROUPS + g, tm, stride=NSA_ROWS), :] = rows[kind]
        for kind in range(2):
            win_ref[pl.ds(kind * NSA_KV_GROUPS + g, tm, stride=WIN_ROWS), :] = rows[4 + kind]


def _prep(z, cos, sin, gq, gk, off_kv):
    n = z.shape[0]
    tm = min(n, 256)
    nt = cos.shape[0] // tm
    kvb = off_kv // (2 * KV_W)
    return pl.pallas_call(
        functools.partial(_prep_body, tm=tm),
        grid=(n // tm,),
        in_specs=[pl.BlockSpec((tm, NSA_W), lambda i: (i, OFF_QA // NSA_W)),
                  pl.BlockSpec((tm, 2 * KV_W), lambda i: (i, kvb)),
                  pl.BlockSpec((tm, 2 * KV_W), lambda i: (i, kvb + 1)),
                  pl.BlockSpec((tm, 2 * KV_W), lambda i: (i, kvb + 2)),
                  pl.BlockSpec((tm, HEAD_DIM), lambda i: (i % nt, 0)),
                  pl.BlockSpec((tm, HEAD_DIM), lambda i: (i % nt, 0)),
                  pl.BlockSpec((1, HEAD_DIM), lambda i: (0, 0)),
                  pl.BlockSpec((3, HEAD_DIM), lambda i: (0, 0))],
        out_specs=[pl.BlockSpec((tm, NSA_W), lambda i: (i, 0)),
                   pl.BlockSpec((tm, NSA_W), lambda i: (i, 0)),
                   pl.BlockSpec((tm, 6 * KV_W), lambda i: (i, 0)),
                   pl.BlockSpec((tm, 2 * KV_W), lambda i: (i, 0)),
                   pl.BlockSpec((tm * NSA_ROWS, HEAD_DIM), lambda i: (i, 0)),
                   pl.BlockSpec((tm * WIN_ROWS, HEAD_DIM), lambda i: (i, 0))],
        out_shape=[jax.ShapeDtypeStruct((n, NSA_W), BF16),
                   jax.ShapeDtypeStruct((n, NSA_W), BF16),
                   jax.ShapeDtypeStruct((n, 6 * KV_W), BF16),
                   jax.ShapeDtypeStruct((n, 2 * KV_W), F32),
                   jax.ShapeDtypeStruct((n * NSA_ROWS, HEAD_DIM), F32),
                   jax.ShapeDtypeStruct((n * WIN_ROWS, HEAD_DIM), F32)],
        compiler_params=_params(("parallel",), 32),
        name="prep",
    )(z, z, z, z, cos, sin, gq, gk)


def _rope_tables(pos):
    half = HEAD_DIM // 2
    inv = ROPE_THETA ** (-jnp.arange(half, dtype=F32) / half)
    ang = pos.astype(F32)[:, None] * inv[None, :]
    cos = jnp.cos(ang)
    sin = jnp.sin(ang)
    return jnp.concatenate([cos, cos], axis=-1), jnp.concatenate([-sin, sin], axis=-1)


def _cmp_hidden(p, q, b1):
    return _silu(p + pltpu.roll(q, q.shape[0] - 1, 0) + b1)


def _compress_body(rows_ref, pos_ref, w1_ref, b1_ref, w2_ref, o_ref, *, nr):
    r = jnp.concatenate([_rows(rows_ref, l, nr, CMP_STRIDE) for l in range(CMP_STRIDE)], axis=1)
    half = CMP_STRIDE * HEAD_DIM
    p = _dot((r + pos_ref[0:1, :]).astype(BF16), w1_ref[0:half, :])
    q = _dot((r + pos_ref[1:2, :]).astype(BF16), w1_ref[half:2 * half, :])
    hid = _cmp_hidden(p, q, b1_ref[...])
    o_ref[...] = _dot(hid.astype(BF16), w2_ref[...])


def _compress(cmp_in, b, t, pos, w1, b1, w2, l):
    nr = t // CMP_STRIDE
    half = CMP_STRIDE * HEAD_DIM
    return pl.pallas_call(
        functools.partial(_compress_body, nr=nr),
        grid=(b, NSA_KV_GROUPS, 2),
        in_specs=[pl.BlockSpec((t, HEAD_DIM), lambda i, g, k: (i, k * NSA_KV_GROUPS + g)),
                  pl.BlockSpec((None, None, 2, half), lambda i, g, k: (l, k, 0, 0)),
                  pl.BlockSpec((None, None, 2 * half, HEAD_DIM), lambda i, g, k: (l, k, 0, 0)),
                  pl.BlockSpec((None, None, 1, HEAD_DIM), lambda i, g, k: (l, k, 0, 0)),
                  pl.BlockSpec((None, None, HEAD_DIM, HEAD_DIM), lambda i, g, k: (l, k, 0, 0))],
        out_specs=pl.BlockSpec((None, None, None, nr, HEAD_DIM), lambda i, g, k: (i, g, k, 0, 0)),
        out_shape=jax.ShapeDtypeStruct((b, NSA_KV_GROUPS, 2, nr, HEAD_DIM), F32),
        compiler_params=_params(("parallel", "parallel", "parallel"), 32),
        name="compress",
    )(cmp_in, pos, w1, b1, w2)


def _block_scores(imp, blk, t, n_sel):
    tb = t >> (SEL_BLOCK.bit_length() - 1)
    forced = (blk == 0) | (blk == tb) | (blk == tb - 1)
    future = blk * SEL_BLOCK > t
    score = jnp.where(future, NEG_BIG, imp + SEL_BONUS * jnp.where(forced, 1.0, 0.0))
    return jnp.where(blk < n_sel, score, BELOW_NEG_BIG)


def _nsa_body(qn_ref, qr_ref, ga_ref, ck_ref, cv_ref, ks_ref, vs_ref, kw_ref, vw_ref, ovt_ref, et_ref,
              o_ref, sc_ref, *, tq, tk, n_cmp, n_sel, k_top, wlen):
    j = pl.program_id(2)
    t0 = j * tq
    t_col = t0 + _iota((tq, 1), 0)
    t_row = t0 + _iota((1, tq), 1)
    nc = ck_ref.shape[0]
    nsr = ovt_ref.shape[0]
    nsp = et_ref.shape[1]

    ck = ck_ref[...].astype(BF16)
    cv = cv_ref[...].astype(BF16)
    n_idx = _iota((1, nc), 1)
    cmask = (n_idx * CMP_STRIDE + (CMP_BLOCK - 1) <= t_col) & (n_idx < n_cmp)
    psum = jnp.zeros((tq, nc), F32)
    o_c = []
    for h in range(HEADS_PER_GROUP):
        q = qn_ref[:, h * HEAD_DIM:(h + 1) * HEAD_DIM]
        p = _masked_softmax(_dot_nt(q, ck), cmask)
        o_c.append(_dot(p.astype(BF16), cv))
        psum = psum + p

    ovt = ovt_ref[...]
    imp_t = sum(_dot_nt(ovt, part) for part in _split3(psum))
    blk = _iota((nsr, 1), 0)
    score = _block_scores(imp_t, blk, t_row, n_sel)
    sc_ref[...] = score

    def rank_step(sp, rank):
        row = sc_ref[pl.ds(sp, 1), :]
        return rank + jnp.where(sp < blk, jnp.where(row >= score, 1.0, 0.0), jnp.where(row > score, 1.0, 0.0))

    rank = lax.fori_loop(0, n_sel, rank_step, jnp.zeros((nsr, tq), F32))
    chosen_t = jnp.where(rank < k_top, 1.0, 0.0)
    if nsp > nsr:
        chosen_t = jnp.concatenate([chosen_t, jnp.zeros((nsp - nsr, tq), F32)], axis=0)
    sel = chosen_t.T.astype(BF16)

    qr = [qr_ref[:, h * HEAD_DIM:(h + 1) * HEAD_DIM] for h in range(HEADS_PER_GROUP)]

    def sel_step(kb, carry):
        ms, ls, accs = carry
        k0 = pl.multiple_of(kb * tk, tk)
        k = ks_ref[pl.ds(k0, tk), :]
        v = vs_ref[pl.ds(k0, tk), :]
        chosen = _dot_nt(sel, et_ref[pl.ds(k0, tk), :])
        bias = jnp.where((chosen > 0.5) & (k0 + _iota((1, tk), 1) <= t_col), 0.0, NEG_BIG)
        new_ms, new_ls, new_accs = [], [], []
        for h in range(HEADS_PER_GROUP):
            s = _dot_nt(qr[h], k) + bias
            m_new = jnp.maximum(ms[h], jnp.max(s, axis=-1, keepdims=True))
            p = jnp.exp(s - m_new)
            alpha = jnp.exp(ms[h] - m_new)
            new_ms.append(m_new)
            new_ls.append(alpha * ls[h] + jnp.sum(p, axis=-1, keepdims=True))
            new_accs.append(alpha * accs[h] + _dot(p.astype(BF16), v))
        return tuple(new_ms), tuple(new_ls), tuple(new_accs)

    init = (tuple(jnp.full((tq, 1), NEG_BIG, F32) for _ in range(HEADS_PER_GROUP)),
            tuple(jnp.zeros((tq, 1), F32) for _ in range(HEADS_PER_GROUP)),
            tuple(jnp.zeros((tq, HEAD_DIM), F32) for _ in range(HEADS_PER_GROUP)))
    n_kt = (t0 + tq + tk - 1) // tk
    _, ls, accs = lax.fori_loop(0, n_kt, sel_step, init)
    o_s = [accs[h] / jnp.where(ls[h] > 0.0, ls[h], 1.0) for h in range(HEADS_PER_GROUP)]

    w0 = pl.multiple_of(jnp.maximum(t0 + tq - wlen, 0), tq)
    kw = kw_ref[pl.ds(w0, wlen), :]
    vw = vw_ref[pl.ds(w0, wlen), :]
    dt = t_col - (w0 + _iota((1, wlen), 1))
    wbias = jnp.where((dt >= 0) & (dt < WINDOW), 0.0, NEG_BIG)

    gates = jax.nn.sigmoid(ga_ref[...])
    for h in range(HEADS_PER_GROUP):
        s = _dot_nt(qr[h], kw) + wbias
        p = jnp.exp(s - jnp.max(s, axis=-1, keepdims=True))
        o_w = _dot(p.astype(BF16), vw) / jnp.sum(p, axis=-1, keepdims=True)
        g_c = gates[:, 3 * h:3 * h + 1]
        g_s = gates[:, 3 * h + 1:3 * h + 2]
        g_w = gates[:, 3 * h + 2:3 * h + 3]
        o_ref[:, h * HEAD_DIM:(h + 1) * HEAD_DIM] = g_c * o_c[h] + g_s * o_s[h] + g_w * o_w


def _sel_geometry(t_all, nc_pad, ns_pad):
    n_cmp = (t_all - CMP_BLOCK) // CMP_STRIDE + 1
    n_sel = -(-t_all // SEL_BLOCK)
    cmp_start = np.arange(n_cmp) * CMP_STRIDE
    sel_start = np.arange(n_sel) * SEL_BLOCK
    overlap = ((cmp_start[:, None] < sel_start[None, :] + SEL_BLOCK)
               & (cmp_start[:, None] + CMP_BLOCK > sel_start[None, :])).astype(np.float32)
    ovt = np.zeros((ns_pad, nc_pad), np.float32)
    ovt[:n_sel, :n_cmp] = overlap.T
    return n_cmp, n_sel, ovt


def _nsa_prompt(qn, qr, z, ckv, kv16, b, t, off_ga):
    tq = 128
    tk = min(512, t)
    nq = t // tq
    nc = t // CMP_STRIDE
    n_sel = -(-t // SEL_BLOCK)
    ns_rows = -(-n_sel // SUBLANES) * SUBLANES
    ns_pad = -(-n_sel // LANES) * LANES
    n_cmp, n_sel, ovt = _sel_geometry(t, nc, ns_rows)
    k_top = min(N_SELECT, n_sel)
    wlen = min(WINDOW + tq, t)
    et = (np.arange(t)[:, None] // SEL_BLOCK == np.arange(ns_pad)[None, :]).astype(np.float32)
    gab = off_ga // LANES
    kernel = functools.partial(_nsa_body, tq=tq, tk=tk, n_cmp=n_cmp, n_sel=n_sel, k_top=k_top, wlen=wlen)
    gw = HEADS_PER_GROUP * HEAD_DIM
    g_n = NSA_KV_GROUPS
    return pl.pallas_call(
        kernel,
        grid=(b, g_n, nq),
        in_specs=[pl.BlockSpec((tq, gw), lambda i, g, j: (i * nq + j, g)),
                  pl.BlockSpec((tq, gw), lambda i, g, j: (i * nq + j, g)),
                  pl.BlockSpec((tq, LANES), lambda i, g, j: (i * nq + j, gab + g)),
                  pl.BlockSpec((None, None, None, nc, HEAD_DIM), lambda i, g, j: (i, g, 0, 0, 0)),
                  pl.BlockSpec((None, None, None, nc, HEAD_DIM), lambda i, g, j: (i, g, 1, 0, 0)),
                  pl.BlockSpec((t, HEAD_DIM), lambda i, g, j: (i, 2 * g_n + g)),
                  pl.BlockSpec((t, HEAD_DIM), lambda i, g, j: (i, 3 * g_n + g)),
                  pl.BlockSpec((t, HEAD_DIM), lambda i, g, j: (i, 4 * g_n + g)),
                  pl.BlockSpec((t, HEAD_DIM), lambda i, g, j: (i, 5 * g_n + g)),
                  pl.BlockSpec((ns_rows, nc), lambda i, g, j: (0, 0)),
                  pl.BlockSpec((t, ns_pad), lambda i, g, j: (0, 0))],
        out_specs=pl.BlockSpec((tq, gw), lambda i, g, j: (i * nq + j, g)),
        out_shape=jax.ShapeDtypeStruct((b * t, NSA_W), F32),
        scratch_shapes=[pltpu.VMEM((ns_rows, tq), F32)],
        compiler_params=_params(("parallel", "parallel", "arbitrary"), 48),
        name="nsa_prompt",
    )(qn, qr, z, ckv, ckv, kv16, kv16, kv16, kv16, jnp.asarray(ovt, BF16), jnp.asarray(et, BF16))


def _sb_body(q_ref, k_ref, v_ref, u_ref, o_ref, *, tq, nh):
    j = pl.program_id(2)
    t_col = j * tq + _iota((tq, 1), 0)
    qs = [q_ref[:, h * HEAD_DIM:(h + 1) * HEAD_DIM].astype(BF16) for h in range(nh)]
    u = u_ref[...]

    def live(carry):
        i, _, _, c_max = carry
        return jnp.logical_and(i <= j, c_max > -SB_UNDERFLOW)

    def step(carry):
        i, cs, accs, _ = carry
        k0 = pl.multiple_of((j - i) * tq, tq)
        mask = k0 + _iota((1, tq), 1) < t_col
        new_cs, new_accs = [], []
        for h in range(nh):
            sl = slice(h * HEAD_DIM, (h + 1) * HEAD_DIM)
            k = k_ref[pl.ds(k0, tq), sl].astype(BF16)
            v = v_ref[pl.ds(k0, tq), sl].astype(BF16)
            z = _dot_nt(qs[h], k)
            sp = _softplus(z)
            log1m = jnp.where(mask, -sp, 0.0)
            hi, lo = _split2(log1m)
            suffix = _dot(hi, u) + _dot(lo, u)
            a = jnp.where(mask, jnp.exp(z - sp + cs[h] + suffix - log1m), 0.0)
            new_cs.append(cs[h] + suffix[:, 0:1])
            new_accs.append(accs[h] + _dot(a.astype(BF16), v))
        c_max = functools.reduce(jnp.maximum, [jnp.max(c) for c in new_cs])
        return i + 1, tuple(new_cs), tuple(new_accs), c_max

    init = (jnp.int32(0), tuple(jnp.zeros((tq, 1), F32) for _ in range(nh)),
            tuple(jnp.zeros((tq, HEAD_DIM), F32) for _ in range(nh)), jnp.float32(0.0))
    _, _, accs, _ = lax.while_loop(live, step, init)
    for h in range(nh):
        o_ref[:, h * HEAD_DIM:(h + 1) * HEAD_DIM] = accs[h]


def _sb_prompt(z, b, t):
    tq = 128
    nh = 4
    nq = t // tq
    gw = nh * HEAD_DIM
    u = (np.arange(tq)[:, None] >= np.arange(tq)[None, :]).astype(np.float32)
    qb, kb, vb = OFF_QB // gw, OFF_KB // gw, OFF_VB // gw
    return pl.pallas_call(
        functools.partial(_sb_body, tq=tq, nh=nh),
        grid=(b, SB_HEADS // nh, nq),
        in_specs=[pl.BlockSpec((tq, gw), lambda i, h, j: (i * nq + j, qb + h)),
                  pl.BlockSpec((t, gw), lambda i, h, j: (i, kb + h)),
                  pl.BlockSpec((t, gw), lambda i, h, j: (i, vb + h)),
                  pl.BlockSpec((tq, tq), lambda i, h, j: (0, 0))],
        out_specs=pl.BlockSpec((tq, gw), lambda i, h, j: (i * nq + j, h)),
        out_shape=jax.ShapeDtypeStruct((b * t, SB_W), F32),
        compiler_params=_params(("parallel", "parallel", "arbitrary"), 48),
        name="sb_prompt",
    )(z, z, z, jnp.asarray(u, BF16))


def _mix_body(oa_ref, ob_ref, za_ref, zb_ref, m0_ref, m1_ref, x_ref, wa_ref, wb_ref, wo_ref, h_ref):
    ua = (oa_ref[...] * _silu(za_ref[...])).astype(BF16)
    ub = (ob_ref[...] * _silu(zb_ref[...])).astype(BF16)
    ya = _dot(ua, wa_ref[...])
    yb = _dot(ub, wb_ref[...])
    mix = jax.nn.sigmoid(m0_ref[...]) * ya + jax.nn.sigmoid(m1_ref[...]) * yb
    h_ref[...] = x_ref[...] + _dot(mix.astype(BF16), wo_ref[...])


def _mix(o_a, o_b, z, x, w_up_a, w_up_b, w_out, l):
    n, d = x.shape
    tm = min(n, 256)
    once = pl.Buffered(1)
    return pl.pallas_call(
        _mix_body,
        grid=(n // tm,),
        in_specs=[pl.BlockSpec((tm, NSA_W), lambda i: (i, 0)),
                  pl.BlockSpec((tm, SB_W), lambda i: (i, 0)),
                  pl.BlockSpec((tm, NSA_W), lambda i: (i, OFF_ZA // NSA_W)),
                  pl.BlockSpec((tm, SB_W), lambda i: (i, OFF_ZB // SB_W)),
                  pl.BlockSpec((tm, d), lambda i: (i, OFF_M0 // d)),
                  pl.BlockSpec((tm, d), lambda i: (i, OFF_M0 // d + 1)),
                  pl.BlockSpec((tm, d), lambda i: (i, 0)),
                  pl.BlockSpec((None, NSA_W, d), lambda i: (l, 0, 0), pipeline_mode=once),
                  pl.BlockSpec((None, SB_W, d), lambda i: (l, 0, 0), pipeline_mode=once),
                  pl.BlockSpec((None, d, d), lambda i: (l, 0, 0), pipeline_mode=once)],
        out_specs=pl.BlockSpec((tm, d), lambda i: (i, 0)),
        out_shape=jax.ShapeDtypeStruct((n, d), F32),
        compiler_params=_params(("parallel",), 56),
        name="mix",
    )(o_a, o_b, z, z, z, z, x, w_up_a, w_up_b, w_out)


def _ple_body(h_ref, p_ref, g_ref, wpg_ref, wple_ref, o_ref):
    h = h_ref[...]
    hn = (h * lax.rsqrt(jnp.mean(h * h, axis=-1, keepdims=True) + RMS_EPS) * g_ref[...]).astype(BF16)
    gate = jax.nn.sigmoid(_dot(hn, wpg_ref[...]))
    o_ref[...] = h + gate * _dot(p_ref[...].astype(BF16), wple_ref[...])


def _ple(h, p, g, w_pg, w_ple, l):
    n, d = h.shape
    pd = p.shape[1]
    tm = min(n, 256)
    once = pl.Buffered(1)
    return pl.pallas_call(
        _ple_body,
        grid=(n // tm,),
        in_specs=[pl.BlockSpec((tm, d), lambda i: (i, 0)),
                  pl.BlockSpec((tm, pd), lambda i: (i, 0)),
                  pl.BlockSpec((1, d), lambda i: (0, 0)),
                  pl.BlockSpec((None, d, d), lambda i: (l, 0, 0), pipeline_mode=once),
                  pl.BlockSpec((None, pd, d), lambda i: (l, 0, 0), pipeline_mode=once)],
        out_specs=pl.BlockSpec((tm, d), lambda i: (i, 0)),
        out_shape=jax.ShapeDtypeStruct((n, d), F32),
        compiler_params=_params(("parallel",), 40),
        name="ple",
    )(h, p, g, w_pg, w_ple)


def _sbdec_body(pt_ref, qm_ref, cache_ref, tri_ref, ex_ref, o_ref, buf_ref, sem_ref, *, l, n_pages):
    i = pl.program_id(0)

    def page_copy(p, slot):
        page = pt_ref[i * n_pages + n_pages - 1 - p]
        return pltpu.make_async_copy(cache_ref.at[l, page], buf_ref.at[slot], sem_ref.at[slot])

    page_copy(0, 0).start()
    tri = tri_ref[...]
    head_lane = _iota((1, LANES), 1) < SB_HEADS

    def live(carry):
        p, _, _, c_max = carry
        return jnp.logical_and(p < n_pages, c_max > -SB_UNDERFLOW)

    def step(carry):
        p, c, acc, _ = carry
        slot = lax.rem(p, 2)
        page_copy(p, slot).wait()

        @pl.when(p + 1 < n_pages)
        def _():
            page_copy(p + 1, 1 - slot).start()

        page = buf_ref.at[slot]
        z = sum(_dot(_rows(page, h, PAGE_SIZE, SB_ROWS).astype(BF16), qm_ref[h * HEAD_DIM:(h + 1) * HEAD_DIM, :])
                for h in range(SB_HEADS))
        sp = _softplus(z)
        log1m = -sp
        hi, lo = _split2(log1m)
        suffix = _dot(tri, hi) + _dot(tri, lo)
        a = jnp.exp(z - sp + c + suffix - log1m)
        a_wide = _dot(a.astype(BF16), ex_ref[...])
        upd = jnp.concatenate(
            [jnp.sum(a_wide[:, h * HEAD_DIM:(h + 1) * HEAD_DIM] * _rows(page, SB_HEADS + h, PAGE_SIZE, SB_ROWS),
                     axis=0, keepdims=True) for h in range(SB_HEADS)], axis=1)
        c = c + suffix[0:1, :]
        c_max = jnp.max(jnp.where(head_lane, c, -jnp.inf))
        return p + 1, c, acc + upd, c_max

    init = (jnp.int32(0), jnp.zeros((1, LANES), F32), jnp.zeros((1, SB_W), F32), jnp.float32(0.0))
    p_end, _, acc, _ = lax.while_loop(live, step, init)

    @pl.when(p_end < n_pages)
    def _():
        page_copy(p_end, lax.rem(p_end, 2)).wait()

    o_ref[...] = acc


def _sb_sample(q_b, cache_sb, page_table, l):
    b, n_pages = page_table.shape
    head_of_row = np.arange(SB_W) // HEAD_DIM
    onehot = (head_of_row[:, None] == np.arange(LANES)[None, :])
    qm = jnp.where(jnp.asarray(onehot)[None], q_b[:, :, None], 0.0).astype(BF16)
    tri = (np.arange(PAGE_SIZE)[:, None] <= np.arange(PAGE_SIZE)[None, :]).astype(np.float32)
    grid_spec = pltpu.PrefetchScalarGridSpec(
        num_scalar_prefetch=1,
        grid=(b,),
        in_specs=[pl.BlockSpec((None, SB_W, LANES), lambda i, pt: (i, 0, 0)),
                  pl.BlockSpec(memory_space=pl.ANY),
                  pl.BlockSpec((PAGE_SIZE, PAGE_SIZE), lambda i, pt: (0, 0)),
                  pl.BlockSpec((LANES, SB_W), lambda i, pt: (0, 0))],
        out_specs=pl.BlockSpec((None, 1, SB_W), lambda i, pt: (i, 0, 0)),
        scratch_shapes=[pltpu.VMEM((2, PAGE_SIZE * SB_ROWS, HEAD_DIM), F32),
                        pltpu.SemaphoreType.DMA((2,))])
    out = pl.pallas_call(
        functools.partial(_sbdec_body, l=l, n_pages=n_pages),
        grid_spec=grid_spec,
        out_shape=jax.ShapeDtypeStruct((b, 1, SB_W), F32),
        compiler_params=_params(("arbitrary",), 32),
        name="sb_sample",
    )(page_table.reshape(-1), qm, cache_sb, jnp.asarray(tri, BF16), jnp.asarray(onehot.T, BF16))
    return out.reshape(b, SB_W)


def _nsadec_cmp_body(*refs, n_pc, n_cmp, n_sel, k_top, t_q):
    pt_ref, qn_ref = refs[0], refs[1]
    page_refs = refs[2:2 + n_pc]
    pos_ref, w1_ref, b1_ref, w2_ref, ov_ref, sel_ref, oc_ref, p_ref, q_ref = refs[2 + n_pc:]
    del pt_ref
    c = pl.program_id(1)
    rows_pp = PAGE_SIZE // CMP_STRIDE
    half = CMP_STRIDE * HEAD_DIM
    n_chunk = n_pc * rows_pp

    def block_rows(page_ref, kg):
        return jnp.concatenate(
            [_rows(page_ref, l * NSA_ROWS + kg, rows_pp, CMP_STRIDE * NSA_ROWS) for l in range(CMP_STRIDE)],
            axis=1)

    for kind in range(2):
        r = jnp.concatenate([block_rows(page_refs[i], kind * NSA_KV_GROUPS + g)
                             for g in range(NSA_KV_GROUPS) for i in range(n_pc)], axis=0)
        p = _dot((r + pos_ref[kind, 0:1, :]).astype(BF16), w1_ref[kind, 0:half, :])
        q = _dot((r + pos_ref[kind, 1:2, :]).astype(BF16), w1_ref[kind, half:2 * half, :])
        r0 = pl.multiple_of(c * n_chunk, n_chunk)
        for g in range(NSA_KV_GROUPS):
            p_ref[kind, g, pl.ds(r0, n_chunk), :] = p[g * n_chunk:(g + 1) * n_chunk]
            q_ref[kind, g, pl.ds(r0, n_chunk), :] = q[g * n_chunk:(g + 1) * n_chunk]

    @pl.when(c == pl.num_programs(1) - 1)
    def _():
        nc = p_ref.shape[2]
        nsp = ov_ref.shape[1]
        qn = qn_ref[...]
        head_group = _iota((NSA_HEADS, 1), 0) // HEADS_PER_GROUP
        n_idx = _iota((1, nc), 1)
        cmask = (n_idx * CMP_STRIDE + (CMP_BLOCK - 1) <= t_q) & (n_idx < n_cmp)
        blk = _iota((1, nsp), 1)
        o_c = jnp.zeros((NSA_HEADS, HEAD_DIM), F32)
        sel_rows = jnp.zeros((NSA_HEADS, nsp), F32)
        for g in range(NSA_KV_GROUPS):
            ckv = []
            for kind in range(2):
                hid = _cmp_hidden(p_ref[kind, g], q_ref[kind, g], b1_ref[kind])
                ckv.append(_dot(hid.astype(BF16), w2_ref[kind]).astype(BF16))
            p = _masked_softmax(_dot_nt(qn, ckv[0]), cmask)
            mine = head_group == g
            o_c = jnp.where(mine, _dot(p.astype(BF16), ckv[1]), o_c)
            psum = jnp.sum(jnp.where(mine, p, 0.0), axis=0, keepdims=True)
            psum = jnp.broadcast_to(psum, (SUBLANES, nc))
            imp = sum(_dot(part, ov_ref[...]) for part in _split3(psum))[0:1, :]
            score = _block_scores(imp, blk, t_q, n_sel)
            along_lanes = jnp.broadcast_to(score, (nsp, nsp))
            along_rows = along_lanes.T
            beats = jnp.where(_iota((nsp, nsp), 0) < _iota((nsp, nsp), 1),
                              jnp.where(along_rows >= along_lanes, 1.0, 0.0),
                              jnp.where(along_rows > along_lanes, 1.0, 0.0))
            rank = jnp.sum(beats, axis=0, keepdims=True)
            chosen = jnp.where(rank < k_top, 1.0, 0.0)
            sel_rows = jnp.where(_iota((NSA_HEADS, 1), 0) == g, chosen, sel_rows)
        sel_ref[...] = sel_rows
        oc_ref[...] = o_c


def _nsa_sample_select(qn, cache_nsa, page_table, pos, w1, b1, w2, l, past_len):
    b, n_pages = page_table.shape
    n_pc = min(8, n_pages)
    t_all = past_len + 1
    nc = past_len // CMP_STRIDE
    ns_pad = -(-(-(-t_all // SEL_BLOCK)) // LANES) * LANES
    n_cmp, n_sel, ovt = _sel_geometry(t_all, nc, ns_pad)
    k_top = min(N_SELECT, n_sel)
    half = CMP_STRIDE * HEAD_DIM
    kernel = functools.partial(_nsadec_cmp_body, n_pc=n_pc, n_cmp=n_cmp, n_sel=n_sel, k_top=k_top,
                               t_q=past_len)

    def page_spec(i_pc):
        return pl.BlockSpec((None, None, PAGE_SIZE * NSA_ROWS, HEAD_DIM),
                            lambda i, c, pt: (l, pt[i * n_pages + c * n_pc + i_pc], 0, 0))

    grid_spec = pltpu.PrefetchScalarGridSpec(
        num_scalar_prefetch=1,
        grid=(b, n_pages // n_pc),
        in_specs=[pl.BlockSpec((None, NSA_HEADS, HEAD_DIM), lambda i, c, pt: (i, 0, 0))]
        + [page_spec(i_pc) for i_pc in range(n_pc)]
        + [pl.BlockSpec((None, 2, 2, half), lambda i, c, pt: (l, 0, 0, 0)),
           pl.BlockSpec((None, 2, 2 * half, HEAD_DIM), lambda i, c, pt: (l, 0, 0, 0)),
           pl.BlockSpec((None, 2, 1, HEAD_DIM), lambda i, c, pt: (l, 0, 0, 0)),
           pl.BlockSpec((None, 2, HEAD_DIM, HEAD_DIM), lambda i, c, pt: (l, 0, 0, 0)),
           pl.BlockSpec((nc, ns_pad), lambda i, c, pt: (0, 0))],
        out_specs=[pl.BlockSpec((None, NSA_HEADS, ns_pad), lambda i, c, pt: (i, 0, 0)),
                   pl.BlockSpec((None, NSA_HEADS, HEAD_DIM), lambda i, c, pt: (i, 0, 0))],
        scratch_shapes=[pltpu.VMEM((2, NSA_KV_GROUPS, nc, HEAD_DIM), F32),
                        pltpu.VMEM((2, NSA_KV_GROUPS, nc, HEAD_DIM), F32)])
    sel, o_c = pl.pallas_call(
        kernel,
        grid_spec=grid_spec,
        out_shape=[jax.ShapeDtypeStruct((b, NSA_HEADS, ns_pad), F32),
                   jax.ShapeDtypeStruct((b, NSA_HEADS, HEAD_DIM), F32)],
        compiler_params=_params(("parallel", "arbitrary"), 48),
        name="nsa_sample_select",
    )(page_table.reshape(-1), qn, *([cache_nsa] * n_pc), pos, w1, b1, w2,
      jnp.asarray(ovt.T, BF16))
    return sel, o_c, n_sel, k_top


def _nsadec_att_body(idx_ref, pt_ref, qr_ref, gate_ref, oc_ref, blk_ref, new_ref, wnew_ref, win_ref,
                     o_ref, m_ref, l_ref, acc_ref, *, k_top, n_sel):
    del pt_ref
    i = pl.program_id(0)
    g = pl.program_id(1)
    s = pl.program_id(2)
    q = qr_ref[...]
    base = (i * NSA_KV_GROUPS + g) * k_top
    g_n = NSA_KV_GROUPS

    def of_group(ref, kind, count, stride):
        if count == 1:
            return jnp.where(g == 0, ref[kind * g_n:kind * g_n + 1, :], ref[kind * g_n + 1:kind * g_n + 2, :])
        return jnp.where(g == 0, _rows(ref, kind * g_n, count, stride), _rows(ref, kind * g_n + 1, count, stride))

    @pl.when(s == 0)
    def _():
        m_ref[...] = jnp.full_like(m_ref, NEG_BIG)
        l_ref[...] = jnp.zeros_like(l_ref)
        acc_ref[...] = jnp.zeros_like(acc_ref)

    @pl.when(idx_ref[base + s] < n_sel - 1)
    def _():
        k = of_group(blk_ref, 2, SEL_BLOCK, NSA_ROWS).astype(BF16)
        v = of_group(blk_ref, 3, SEL_BLOCK, NSA_ROWS).astype(BF16)
        sc = _dot_nt(q, k)
        m_new = jnp.maximum(m_ref[...], jnp.max(sc, axis=-1, keepdims=True))
        p = jnp.exp(sc - m_new)
        alpha = jnp.exp(m_ref[...] - m_new)
        l_ref[...] = alpha * l_ref[...] + jnp.sum(p, axis=-1, keepdims=True)
        acc_ref[...] = alpha * acc_ref[...] + _dot(p.astype(BF16), v)
        m_ref[...] = m_new

    @pl.when(s == k_top - 1)
    def _():
        qf = q.astype(F32)
        n_new = lax.fori_loop(
            0, k_top, lambda r, a: a + (idx_ref[base + r] == n_sel - 1).astype(jnp.int32), jnp.int32(0))
        has_new = n_new > 0
        s_new = jnp.sum(qf * of_group(new_ref, 2, 1, 1), axis=-1, keepdims=True)
        s_new = jnp.where(has_new, s_new, NEG_BIG)
        m_new = jnp.maximum(m_ref[...], s_new)
        p_new = jnp.where(has_new, jnp.exp(s_new - m_new), 0.0)
        alpha = jnp.exp(m_ref[...] - m_new)
        l_s = alpha * l_ref[...] + p_new
        o_s = (alpha * acc_ref[...] + p_new * of_group(new_ref, 3, 1, 1)) / jnp.where(l_s > 0.0, l_s, 1.0)

        w_buf = win_ref.shape[0] // WIN_ROWS
        sw = _dot_nt(q, of_group(win_ref, 0, w_buf, WIN_ROWS).astype(BF16))
        wmask = (w_buf - _iota((1, w_buf), 1)) < WINDOW
        sw = jnp.where(wmask, sw, NEG_BIG)
        sw_new = jnp.sum(qf * of_group(wnew_ref, 0, 1, 1), axis=-1, keepdims=True)
        mw = jnp.maximum(jnp.max(sw, axis=-1, keepdims=True), sw_new)
        pw = jnp.where(wmask, jnp.exp(sw - mw), 0.0)
        pw_new = jnp.exp(sw_new - mw)
        l_w = jnp.sum(pw, axis=-1, keepdims=True) + pw_new
        o_w = (_dot(pw.astype(BF16), of_group(win_ref, 1, w_buf, WIN_ROWS).astype(BF16))
               + pw_new * of_group(wnew_ref, 1, 1, 1)) / l_w

        gates = jax.nn.sigmoid(gate_ref[...])
        o_ref[...] = gates[0] * oc_ref[...] + gates[1] * o_s + gates[2] * o_w


def _nsa_sample_attend(idx, qr, gates, o_c, cache_nsa, page_table, nsa_new, win_new, state_win, l,
                       n_sel, k_top):
    b, n_pages = page_table.shape
    per_page = PAGE_SIZE // SEL_BLOCK
    last = n_sel - 2
    g_n = NSA_KV_GROUPS
    blk_rows = SEL_BLOCK * NSA_ROWS
    win_rows = state_win.shape[2]

    def blk_map(i, g, s, idx_ref, pt):
        blk = jnp.minimum(idx_ref[(i * g_n + g) * k_top + s], last)
        return (l, pt[i * n_pages + blk // per_page], blk % per_page, 0, 0)

    grid_spec = pltpu.PrefetchScalarGridSpec(
        num_scalar_prefetch=2,
        grid=(b, g_n, k_top),
        in_specs=[pl.BlockSpec((None, NSA_HEADS, HEAD_DIM), lambda i, g, s, *_: (i, 0, 0)),
                  pl.BlockSpec((None, 3, NSA_HEADS, HEAD_DIM), lambda i, g, s, *_: (i, 0, 0, 0)),
                  pl.BlockSpec((None, NSA_HEADS, HEAD_DIM), lambda i, g, s, *_: (i, 0, 0)),
                  pl.BlockSpec((None, None, None, blk_rows, HEAD_DIM), blk_map),
                  pl.BlockSpec((None, NSA_ROWS, HEAD_DIM), lambda i, g, s, *_: (i, 0, 0)),
                  pl.BlockSpec((None, WIN_ROWS, HEAD_DIM), lambda i, g, s, *_: (i, 0, 0)),
                  pl.BlockSpec((None, None, win_rows, HEAD_DIM), lambda i, g, s, *_: (l, i, 0, 0))],
        out_specs=pl.BlockSpec((None, None, NSA_HEADS, HEAD_DIM), lambda i, g, s, *_: (i, g, 0, 0)),
        scratch_shapes=[pltpu.VMEM((NSA_HEADS, 1), F32), pltpu.VMEM((NSA_HEADS, 1), F32),
                        pltpu.VMEM((NSA_HEADS, HEAD_DIM), F32)])
    return pl.pallas_call(
        functools.partial(_nsadec_att_body, k_top=k_top, n_sel=n_sel),
        grid_spec=grid_spec,
        out_shape=jax.ShapeDtypeStruct((b, g_n, NSA_HEADS, HEAD_DIM), F32),
        compiler_params=_params(("parallel", "parallel", "arbitrary"), 32),
        name="nsa_sample_attend",
    )(idx.reshape(-1), page_table.reshape(-1), qr, gates, o_c, cache_nsa, nsa_new, win_new, state_win)


def _permute_w_in(w_in, d_model):
    o = np.cumsum([0, NSA_W, 6 * KV_W, 3 * NSA_HEADS, NSA_W, 3 * SB_W, SB_W, 2 * d_model])
    depth = w_in.shape[0]
    gate = w_in[:, :, o[2]:o[3]]
    per_group = 3 * HEADS_PER_GROUP
    gate_cols = []
    for g in range(NSA_KV_GROUPS):
        gate_cols.append(gate[:, :, g * per_group:(g + 1) * per_group])
        gate_cols.append(jnp.zeros((depth, d_model, LANES - per_group), w_in.dtype))
    gate_cols.append(jnp.zeros((depth, d_model, GA_W - NSA_KV_GROUPS * LANES), w_in.dtype))
    cols = [w_in[:, :, o[0]:o[1]], w_in[:, :, o[3]:o[4]], w_in[:, :, o[5]:o[6]],
            w_in[:, :, o[4]:o[4] + SB_W] * SCALE, w_in[:, :, o[4] + SB_W:o[5]],
            w_in[:, :, o[6]:o[7]], w_in[:, :, o[1]:o[2]]] + gate_cols
    return jnp.concatenate(cols, axis=-1).astype(BF16)


def kernel(x_prompt, x_sample, cache_nsa, cache_sb, state_win, page_table, p_prompt, p_sample, g_mix,
           w_in, g_q, g_k, cmp_pos, w_c1, b_c1, w_c2, w_up_a, w_up_b, w_out, g_ple, w_pg, w_ple):
    depth = w_in.shape[0]
    bp, t, d = x_prompt.shape
    bs, ts, _ = x_sample.shape
    assert ts == 1
    n_pages = page_table.shape[1]
    past_len = n_pages * PAGE_SIZE
    n_pool = cache_nsa.shape[1]
    w_buf = state_win.shape[2]
    off_m1, off_kv, off_ga, n_cols = _offsets(d)
    half = CMP_STRIDE * HEAD_DIM

    w_proj = _permute_w_in(w_in, d)
    wa, wb, wo = w_up_a.astype(BF16), w_up_b.astype(BF16), w_out.astype(BF16)
    wpg, wple = w_pg.astype(BF16), w_ple.astype(BF16)
    pos2 = cmp_pos.reshape(depth, 2, 2, half)
    w1 = w_c1.reshape(depth, 2, 2 * half, HEAD_DIM).astype(BF16)
    b1 = b_c1.reshape(depth, 2, 1, HEAD_DIM)
    w2 = w_c2.astype(BF16)
    cos_p, sin_p = _rope_tables(jnp.arange(t))
    cos_s, sin_s = _rope_tables(jnp.full((bs,), past_len))
    nsa_pages = cache_nsa.reshape(depth, n_pool, PAGE_SIZE * NSA_ROWS, HEAD_DIM)
    nsa_blocks = cache_nsa.reshape(depth, n_pool, PAGE_SIZE // SEL_BLOCK, SEL_BLOCK * NSA_ROWS, HEAD_DIM)
    sb_pages = cache_sb.reshape(depth, n_pool, PAGE_SIZE * SB_ROWS, HEAD_DIM)
    win_state = state_win.reshape(depth, bs, w_buf * WIN_ROWS, HEAD_DIM)

    hp = x_prompt.reshape(bp * t, d)
    hs = x_sample.reshape(bs, d)
    outs = [[] for _ in range(6)]
    for l in range(depth):
        gm, gq, gp = g_mix[l][None], g_q[l][None], g_ple[l][None]

        z = _proj(hp, gm, w_proj, l)
        qn, qr, kv16, cmp_in, nsa_rows, win_rows = _prep(z, cos_p, sin_p, gq, g_k[l], off_kv)
        ckv = _compress(cmp_in, bp, t, pos2, w1, b1, w2, l)
        o_a = _nsa_prompt(qn, qr, z, ckv, kv16, bp, t, off_ga)
        o_b = _sb_prompt(z, bp, t)
        h1 = _mix(o_a, o_b, z, hp, wa, wb, wo, l)
        hp = _ple(h1, p_prompt[l].reshape(bp * t, -1), gp, wpg, wple, l)
        outs[0].append(nsa_rows.reshape(bp, t, 4, NSA_KV_GROUPS, HEAD_DIM))
        outs[2].append(z[:, OFF_KB:OFF_KB + 2 * SB_W].reshape(bp, t, 2, SB_HEADS, HEAD_DIM))
        keep = min(WINDOW, t)
        outs[4].append(win_rows.reshape(bp, t, 2, NSA_KV_GROUPS, HEAD_DIM)[:, t - keep:])

        zs = _proj(hs, gm, w_proj, l)
        qn_s, qr_s, _, _, nsa_new, win_new = _prep(zs, cos_s, sin_s, gq, g_k[l], off_kv)
        sel, o_c, n_sel, k_top = _nsa_sample_select(
            qn_s.reshape(bs, NSA_HEADS, HEAD_DIM), nsa_pages, page_table, pos2, w1, b1, w2, l, past_len)
        _, idx = lax.top_k(sel[:, :NSA_KV_GROUPS, :], k_top)
        gate_logits = zs[:, off_ga:off_ga + NSA_KV_GROUPS * LANES].reshape(bs, NSA_KV_GROUPS, LANES)
        gate_logits = gate_logits[:, :, :3 * HEADS_PER_GROUP].reshape(bs, NSA_HEADS, 3)
        gate_logits = jnp.broadcast_to(gate_logits.transpose(0, 2, 1)[..., None], (bs, 3, NSA_HEADS, HEAD_DIM))
        o_full = _nsa_sample_attend(idx.astype(jnp.int32), qr_s.reshape(bs, NSA_HEADS, HEAD_DIM), gate_logits,
                                    o_c, nsa_blocks, page_table, nsa_new.reshape(bs, NSA_ROWS, HEAD_DIM),
                                    win_new.reshape(bs, WIN_ROWS, HEAD_DIM), win_state, l, n_sel, k_top)
        o_a_s = jnp.concatenate(
            [o_full[:, g, g * HEADS_PER_GROUP:(g + 1) * HEADS_PER_GROUP] for g in range(NSA_KV_GROUPS)],
            axis=1).reshape(bs, NSA_W)
        o_b_s = _sb_sample(zs[:, OFF_QB:OFF_QB + SB_W], sb_pages, page_table, l)
        h1s = _mix(o_a_s, o_b_s, zs, hs, wa, wb, wo, l)
        hs = _ple(h1s, p_sample[l].reshape(bs, -1), gp, wpg, wple, l)
        outs[1].append(nsa_new.reshape(bs, 1, 4, NSA_KV_GROUPS, HEAD_DIM))
        outs[3].append(zs[:, OFF_KB:OFF_KB + 2 * SB_W].reshape(bs, 1, 2, SB_HEADS, HEAD_DIM))
        new_win = jnp.concatenate([win_state[l], win_new.reshape(bs, WIN_ROWS, HEAD_DIM)], axis=1)[:, WIN_ROWS:]
        outs[5].append(new_win.reshape(bs, w_buf, 2, NSA_KV_GROUPS, HEAD_DIM))

    return (hp.reshape(bp, t, d), hs.reshape(bs, 1, d), jnp.stack(outs[0]), jnp.stack(outs[1]),
            jnp.stack(outs[2]), jnp.stack(outs[3]), jnp.stack(outs[4]), jnp.stack(outs[5]))
```

```python
import functools

import numpy as np
import jax
import jax.numpy as jnp
from jax import lax
from jax.experimental import pallas as pl
from jax.experimental.pallas import tpu as pltpu

HEAD_DIM = 128
NSA_HEADS = 8
NSA_KV_GROUPS = 2
HEADS_PER_GROUP = NSA_HEADS // NSA_KV_GROUPS
SB_HEADS = 8
CMP_BLOCK = 32
CMP_STRIDE = 16
SEL_BLOCK = 64
N_SELECT = 16
WINDOW = 512
ROPE_THETA = 10000.0
PAGE_SIZE = 128
RMS_EPS = 1e-6
SEL_BONUS = 1e3
NEG_BIG = -1e30
BELOW_NEG_BIG = -3e38
SB_UNDERFLOW = 104.0

NSA_W = NSA_HEADS * HEAD_DIM
KV_W = NSA_KV_GROUPS * HEAD_DIM
SB_W = SB_HEADS * HEAD_DIM
NSA_ROWS = 4 * NSA_KV_GROUPS
WIN_ROWS = 2 * NSA_KV_GROUPS
SB_ROWS = 2 * SB_HEADS
LANES = 128
SUBLANES = 8
SCALE = HEAD_DIM ** -0.5
V7X_VMEM_BYTES = 64 * 1024 * 1024

F32 = jnp.float32
BF16 = jnp.bfloat16

OFF_QA = 0
OFF_ZA = NSA_W
OFF_ZB = OFF_ZA + NSA_W
OFF_QB = OFF_ZB + SB_W
OFF_KB = OFF_QB + SB_W
OFF_VB = OFF_KB + SB_W
OFF_M0 = OFF_VB + SB_W
GA_W = 512


def _offsets(d_model):
    off_m1 = OFF_M0 + d_model
    off_kv = off_m1 + d_model
    off_ga = off_kv + 6 * KV_W
    return off_m1, off_kv, off_ga, off_ga + GA_W


def _params(sem, vmem_mb):
    return pltpu.CompilerParams(dimension_semantics=sem,
                                vmem_limit_bytes=min(vmem_mb * 1024 * 1024, V7X_VMEM_BYTES - (4 << 20)))


def _dot(a, b):
    return jnp.dot(a, b, preferred_element_type=F32)


def _dot_nt(a, b):
    return lax.dot_general(a, b, (((1,), (1,)), ((), ())), preferred_element_type=F32)


def _split2(a):
    hi = a.astype(BF16)
    return hi, (a - hi.astype(F32)).astype(BF16)


def _split3(a):
    hi = a.astype(BF16)
    r1 = a - hi.astype(F32)
    mid = r1.astype(BF16)
    lo = (r1 - mid.astype(F32)).astype(BF16)
    return hi, mid, lo


def _iota(shape, dim):
    return lax.broadcasted_iota(jnp.int32, shape, dim)


def _silu(x):
    return x * jax.nn.sigmoid(x)


def _softplus(z):
    return jnp.maximum(z, 0.0) + jnp.log1p(jnp.exp(-jnp.abs(z)))


def _masked_softmax(s, mask):
    s = jnp.where(mask, s, NEG_BIG)
    m = jnp.max(s, axis=-1, keepdims=True)
    p = jnp.where(mask, jnp.exp(s - m), 0.0)
    l = jnp.sum(p, axis=-1, keepdims=True)
    return p / jnp.where(l > 0.0, l, 1.0)


def _rows(ref, start, count, stride):
    return ref[pl.ds(start, count, stride=stride), :]


def _proj_body(x_ref, g_ref, w_ref, o_ref, xn_ref):
    @pl.when(pl.program_id(1) == 0)
    def _():
        x = x_ref[...]
        ms = jnp.mean(x * x, axis=-1, keepdims=True)
        xn_ref[...] = (x * lax.rsqrt(ms + RMS_EPS) * g_ref[...]).astype(BF16)

    o_ref[...] = _dot(xn_ref[...], w_ref[...])


def _proj(x, g, w, l):
    n, d = x.shape
    c = w.shape[2]
    tm = min(n, 1024)
    tn = 512
    return pl.pallas_call(
        _proj_body,
        grid=(n // tm, c // tn),
        in_specs=[pl.BlockSpec((tm, d), lambda i, j: (i, 0)),
                  pl.BlockSpec((1, d), lambda i, j: (0, 0)),
                  pl.BlockSpec((None, d, tn), lambda i, j: (l, 0, j))],
        out_specs=pl.BlockSpec((tm, tn), lambda i, j: (i, j)),
        out_shape=jax.ShapeDtypeStruct((n, c), F32),
        scratch_shapes=[pltpu.VMEM((tm, d), BF16)],
        compiler_params=_params(("parallel", "arbitrary"), 48),
        name="proj",
    )(x, g, w)


def _prep_body(qa_ref, kc_ref, ks_ref, kw_ref, cos_ref, sin_ref, gq_ref, gk_ref,
               qn_ref, qr_ref, kv_ref, cmp_ref, nsa_ref, win_ref, *, tm):
    cos = cos_ref[...]
    sin = sin_ref[...]

    def norm(v, g):
        return v * lax.rsqrt(jnp.mean(v * v, axis=-1, keepdims=True) + RMS_EPS) * g

    def rope(v):
        return v * cos + pltpu.roll(v, HEAD_DIM // 2, 1) * sin

    gq = gq_ref[...]
    for h in range(NSA_HEADS):
        sl = slice(h * HEAD_DIM, (h + 1) * HEAD_DIM)
        qn = norm(qa_ref[:, sl], gq)
        qn_ref[:, sl] = (qn * SCALE).astype(BF16)
        qr_ref[:, sl] = (rope(qn) * SCALE).astype(BF16)
    for g in range(NSA_KV_GROUPS):
        k_sl = slice(g * HEAD_DIM, (g + 1) * HEAD_DIM)
        v_sl = slice(KV_W + g * HEAD_DIM, KV_W + (g + 1) * HEAD_DIM)
        rows = [norm(kc_ref[:, k_sl], gk_ref[0:1, :]), kc_ref[:, v_sl],
                rope(norm(ks_ref[:, k_sl], gk_ref[1:2, :])), ks_ref[:, v_sl],
                rope(norm(kw_ref[:, k_sl], gk_ref[2:3, :])), kw_ref[:, v_sl]]
        for kind, val in enumerate(rows):
            kv_ref[:, kind * KV_W + g * HEAD_DIM:kind * KV_W + (g + 1) * HEAD_DIM] = val.astype(BF16)
        cmp_ref[:, k_sl] = rows[0]
        cmp_ref[:, v_sl] = rows[1]
        for kind in range(4):
            nsa_ref[pl.ds(kind * NSA_KV_GROUPS + g, tm, stride=NSA_ROWS), :] = rows[kind]
        for kind in range(2):
            win_ref[pl.ds(kind * NSA_KV_GROUPS + g, tm, stride=WIN_ROWS), :] = rows[4 + kind]


def _prep(z, cos, sin, gq, gk, off_kv):
    n = z.shape[0]
    tm = min(n, 256)
    nt = cos.shape[0] // tm
    kvb = off_kv // (2 * KV_W)
    return pl.pallas_call(
        functools.partial(_prep_body, tm=tm),
        grid=(n // tm,),
        in_specs=[pl.BlockSpec((tm, NSA_W), lambda i: (i, OFF_QA // NSA_W)),
                  pl.BlockSpec((tm, 2 * KV_W), lambda i: (i, kvb)),
                  pl.BlockSpec((tm, 2 * KV_W), lambda i: (i, kvb + 1)),
                  pl.BlockSpec((tm, 2 * KV_W), lambda i: (i, kvb + 2)),
                  pl.BlockSpec((tm, HEAD_DIM), lambda i: (i % nt, 0)),
                  pl.BlockSpec((tm, HEAD_DIM), lambda i: (i % nt, 0)),
                  pl.BlockSpec((1, HEAD_DIM), lambda i: (0, 0)),
                  pl.BlockSpec((3, HEAD_DIM), lambda i: (0, 0))],
        out_specs=[pl.BlockSpec((tm, NSA_W), lambda i: (i, 0)),
                   pl.BlockSpec((tm, NSA_W), lambda i: (i, 0)),
                   pl.BlockSpec((tm, 6 * KV_W), lambda i: (i, 0)),
                   pl.BlockSpec((tm, 2 * KV_W), lambda i: (i, 0)),
                   pl.BlockSpec((tm * NSA_ROWS, HEAD_DIM), lambda i: (i, 0)),
                   pl.BlockSpec((tm * WIN_ROWS, HEAD_DIM), lambda i: (i, 0))],
        out_shape=[jax.ShapeDtypeStruct((n, NSA_W), BF16),
                   jax.ShapeDtypeStruct((n, NSA_W), BF16),
                   jax.ShapeDtypeStruct((n, 6 * KV_W), BF16),
                   jax.ShapeDtypeStruct((n, 2 * KV_W), F32),
                   jax.ShapeDtypeStruct((n * NSA_ROWS, HEAD_DIM), F32),
                   jax.ShapeDtypeStruct((n * WIN_ROWS, HEAD_DIM), F32)],
        compiler_params=_params(("parallel",), 32),
        name="prep",
    )(z, z, z, z, cos, sin, gq, gk)


def _rope_tables(pos):
    half = HEAD_DIM // 2
    inv = ROPE_THETA ** (-jnp.arange(half, dtype=F32) / half)
    ang = pos.astype(F32)[:, None] * inv[None, :]
    cos = jnp.cos(ang)
    sin = jnp.sin(ang)
    return jnp.concatenate([cos, cos], axis=-1), jnp.concatenate([-sin, sin], axis=-1)


def _cmp_hidden(p, q, b1):
    return _silu(p + pltpu.roll(q, q.shape[0] - 1, 0) + b1)


def _compress_body(rows_ref, pos_ref, w1_ref, b1_ref, w2_ref, o_ref, *, nr):
    r = jnp.concatenate([_rows(rows_ref, l, nr, CMP_STRIDE) for l in range(CMP_STRIDE)], axis=1)
    half = CMP_STRIDE * HEAD_DIM
    p = _dot((r + pos_ref[0:1, :]).astype(BF16), w1_ref[0:half, :])
    q = _dot((r + pos_ref[1:2, :]).astype(BF16), w1_ref[half:2 * half, :])
    hid = _cmp_hidden(p, q, b1_ref[...])
    o_ref[...] = _dot(hid.astype(BF16), w2_ref[...])


def _compress(cmp_in, b, t, pos, w1, b1, w2, l):
    nr = t // CMP_STRIDE
    half = CMP_STRIDE * HEAD_DIM
    return pl.pallas_call(
        functools.partial(_compress_body, nr=nr),
        grid=(b, NSA_KV_GROUPS, 2),
        in_specs=[pl.BlockSpec((t, HEAD_DIM), lambda i, g, k: (i, k * NSA_KV_GROUPS + g)),
                  pl.BlockSpec((None, None, 2, half), lambda i, g, k: (l, k, 0, 0)),
                  pl.BlockSpec((None, None, 2 * half, HEAD_DIM), lambda i, g, k: (l, k, 0, 0)),
                  pl.BlockSpec((None, None, 1, HEAD_DIM), lambda i, g, k: (l, k, 0, 0)),
                  pl.BlockSpec((None, None, HEAD_DIM, HEAD_DIM), lambda i, g, k: (l, k, 0, 0))],
        out_specs=pl.BlockSpec((None, None, None, nr, HEAD_DIM), lambda i, g, k: (i, g, k, 0, 0)),
        out_shape=jax.ShapeDtypeStruct((b, NSA_KV_GROUPS, 2, nr, HEAD_DIM), F32),
        compiler_params=_params(("parallel", "parallel", "parallel"), 32),
        name="compress",
    )(cmp_in, pos, w1, b1, w2)


def _block_scores(imp, blk, t, n_sel):
    tb = t >> (SEL_BLOCK.bit_length() - 1)
    forced = (blk == 0) | (blk == tb) | (blk == tb - 1)
    future = blk * SEL_BLOCK > t
    score = jnp.where(future, NEG_BIG, imp + SEL_BONUS * jnp.where(forced, 1.0, 0.0))
    return jnp.where(blk < n_sel, score, BELOW_NEG_BIG)


def _nsa_body(qn_ref, qr_ref, ga_ref, ck_ref, cv_ref, ks_ref, vs_ref, kw_ref, vw_ref, ovt_ref, et_ref,
              o_ref, sc_ref, *, tq, tk, n_cmp, n_sel, k_top, wlen):
    j = pl.program_id(2)
    t0 = j * tq
    t_col = t0 + _iota((tq, 1), 0)
    t_row = t0 + _iota((1, tq), 1)
    nc = ck_ref.shape[0]
    nsr = ovt_ref.shape[0]
    nsp = et_ref.shape[1]

    ck = ck_ref[...].astype(BF16)
    cv = cv_ref[...].astype(BF16)
    n_idx = _iota((1, nc), 1)
    cmask = (n_idx * CMP_STRIDE + (CMP_BLOCK - 1) <= t_col) & (n_idx < n_cmp)
    nh = HEADS_PER_GROUP

    def stack(ref):
        return jnp.concatenate([ref[:, h * HEAD_DIM:(h + 1) * HEAD_DIM] for h in range(nh)], axis=0)

    def per_head(x):
        return jnp.concatenate([x] * nh, axis=0)

    p = _masked_softmax(_dot_nt(stack(qn_ref), ck), per_head(cmask))
    o_c = _dot(p.astype(BF16), cv)
    psum = sum(p[h * tq:(h + 1) * tq] for h in range(nh))

    ovt = ovt_ref[...]
    imp_t = sum(_dot_nt(ovt, part) for part in _split3(psum))
    blk = _iota((nsr, 1), 0)
    score = _block_scores(imp_t, blk, t_row, n_sel)
    sc_ref[...] = score

    def rank_step(sp, rank):
        row = sc_ref[pl.ds(sp, 1), :]
        return rank + jnp.where(sp < blk, jnp.where(row >= score, 1.0, 0.0), jnp.where(row > score, 1.0, 0.0))

    rank = lax.fori_loop(0, n_sel, rank_step, jnp.zeros((nsr, tq), F32))
    chosen_t = jnp.where(rank < k_top, 1.0, 0.0)
    if nsp > nsr:
        chosen_t = jnp.concatenate([chosen_t, jnp.zeros((nsp - nsr, tq), F32)], axis=0)
    sel = chosen_t.T.astype(BF16)

    qr = stack(qr_ref)

    n_kt = (t0 + tq + tk - 1) // tk

    def scores(kb):
        k0 = pl.multiple_of(kb * tk, tk)
        chosen = _dot_nt(sel, et_ref[pl.ds(k0, tk), :])
        bias = jnp.where((chosen > 0.5) & (k0 + _iota((1, tk), 1) <= t_col), 0.0, NEG_BIG)
        return _dot_nt(qr, ks_ref[pl.ds(k0, tk), :]) + per_head(bias)

    def sel_step(kb, carry):
        s, m, l, acc = carry
        s_next = scores(jnp.minimum(kb + 1, n_kt - 1))
        v = vs_ref[pl.ds(pl.multiple_of(kb * tk, tk), tk), :]
        m_new = jnp.maximum(m, jnp.max(s, axis=-1, keepdims=True))
        p = jnp.exp(s - m_new)
        alpha = jnp.exp(m - m_new)
        return (s_next, m_new, alpha * l + jnp.sum(p, axis=-1, keepdims=True),
                alpha * acc + _dot(p.astype(BF16), v))

    init = (scores(0), jnp.full((nh * tq, 1), NEG_BIG, F32), jnp.zeros((nh * tq, 1), F32),
            jnp.zeros((nh * tq, HEAD_DIM), F32))
    _, _, l_s, acc_s = lax.fori_loop(0, n_kt, sel_step, init)
    o_s = acc_s / jnp.where(l_s > 0.0, l_s, 1.0)

    w0 = pl.multiple_of(jnp.maximum(t0 + tq - wlen, 0), tq)
    kw = kw_ref[pl.ds(w0, wlen), :]
    vw = vw_ref[pl.ds(w0, wlen), :]
    dt = t_col - (w0 + _iota((1, wlen), 1))
    wbias = jnp.where((dt >= 0) & (dt < WINDOW), 0.0, NEG_BIG)

    s = _dot_nt(qr, kw) + per_head(wbias)
    p = jnp.exp(s - jnp.max(s, axis=-1, keepdims=True))
    o_w = _dot(p.astype(BF16), vw) / jnp.sum(p, axis=-1, keepdims=True)

    gates = jax.nn.sigmoid(ga_ref[...])
    for h in range(nh):
        rows = slice(h * tq, (h + 1) * tq)
        g_c = gates[:, 3 * h:3 * h + 1]
        g_s = gates[:, 3 * h + 1:3 * h + 2]
        g_w = gates[:, 3 * h + 2:3 * h + 3]
        o_ref[:, h * HEAD_DIM:(h + 1) * HEAD_DIM] = g_c * o_c[rows] + g_s * o_s[rows] + g_w * o_w[rows]


def _sel_geometry(t_all, nc_pad, ns_pad):
    n_cmp = (t_all - CMP_BLOCK) // CMP_STRIDE + 1
    n_sel = -(-t_all // SEL_BLOCK)
    cmp_start = np.arange(n_cmp) * CMP_STRIDE
    sel_start = np.arange(n_sel) * SEL_BLOCK
    overlap = ((cmp_start[:, None] < sel_start[None, :] + SEL_BLOCK)
               & (cmp_start[:, None] + CMP_BLOCK > sel_start[None, :])).astype(np.float32)
    ovt = np.zeros((ns_pad, nc_pad), np.float32)
    ovt[:n_sel, :n_cmp] = overlap.T
    return n_cmp, n_sel, ovt


def _nsa_prompt(qn, qr, z, ckv, kv16, b, t, off_ga):
    tq = 128
    tk = min(512, t)
    nq = t // tq
    nc = t // CMP_STRIDE
    n_sel = -(-t // SEL_BLOCK)
    ns_rows = -(-n_sel // SUBLANES) * SUBLANES
    ns_pad = -(-n_sel // LANES) * LANES
    n_cmp, n_sel, ovt = _sel_geometry(t, nc, ns_rows)
    k_top = min(N_SELECT, n_sel)
    wlen = min(WINDOW + tq, t)
    et = (np.arange(t)[:, None] // SEL_BLOCK == np.arange(ns_pad)[None, :]).astype(np.float32)
    gab = off_ga // LANES
    kernel = functools.partial(_nsa_body, tq=tq, tk=tk, n_cmp=n_cmp, n_sel=n_sel, k_top=k_top, wlen=wlen)
    gw = HEADS_PER_GROUP * HEAD_DIM
    g_n = NSA_KV_GROUPS
    return pl.pallas_call(
        kernel,
        grid=(b, g_n, nq),
        in_specs=[pl.BlockSpec((tq, gw), lambda i, g, j: (i * nq + j, g)),
                  pl.BlockSpec((tq, gw), lambda i, g, j: (i * nq + j, g)),
                  pl.BlockSpec((tq, LANES), lambda i, g, j: (i * nq + j, gab + g)),
                  pl.BlockSpec((None, None, None, nc, HEAD_DIM), lambda i, g, j: (i, g, 0, 0, 0)),
                  pl.BlockSpec((None, None, None, nc, HEAD_DIM), lambda i, g, j: (i, g, 1, 0, 0)),
                  pl.BlockSpec((t, HEAD_DIM), lambda i, g, j: (i, 2 * g_n + g)),
                  pl.BlockSpec((t, HEAD_DIM), lambda i, g, j: (i, 3 * g_n + g)),
                  pl.BlockSpec((t, HEAD_DIM), lambda i, g, j: (i, 4 * g_n + g)),
                  pl.BlockSpec((t, HEAD_DIM), lambda i, g, j: (i, 5 * g_n + g)),
                  pl.BlockSpec((ns_rows, nc), lambda i, g, j: (0, 0)),
                  pl.BlockSpec((t, ns_pad), lambda i, g, j: (0, 0))],
        out_specs=pl.BlockSpec((tq, gw), lambda i, g, j: (i * nq + j, g)),
        out_shape=jax.ShapeDtypeStruct((b * t, NSA_W), F32),
        scratch_shapes=[pltpu.VMEM((ns_rows, tq), F32)],
        compiler_params=_params(("parallel", "parallel", "arbitrary"), 48),
        name="nsa_prompt",
    )(qn, qr, z, ckv, ckv, kv16, kv16, kv16, kv16, jnp.asarray(ovt, BF16), jnp.asarray(et, BF16))


def _sb_body(q_ref, k_ref, v_ref, u_ref, o_ref, *, tq, tk, nh):
    j = pl.program_id(2)
    t_col = j * tq + _iota((tq, 1), 0)
    qs = [q_ref[:, h * HEAD_DIM:(h + 1) * HEAD_DIM].astype(BF16) for h in range(nh)]
    u = u_ref[...]
    n_tiles = (j + 1) * (tq // tk)

    def live(carry):
        i, _, _, c_max = carry
        return jnp.logical_and(i < n_tiles, c_max > -SB_UNDERFLOW)

    def step(carry):
        i, cs, accs, _ = carry
        k0 = pl.multiple_of((n_tiles - 1 - i) * tk, tk)
        mask = k0 + _iota((1, tk), 1) < t_col
        new_cs, new_accs = [], []
        for h in range(nh):
            sl = slice(h * HEAD_DIM, (h + 1) * HEAD_DIM)
            k = k_ref[pl.ds(k0, tk), sl].astype(BF16)
            v = v_ref[pl.ds(k0, tk), sl].astype(BF16)
            z = _dot_nt(qs[h], k)
            sp = _softplus(z)
            log1m = jnp.where(mask, -sp, 0.0)
            hi, lo = _split2(log1m)
            suffix = _dot(hi, u) + _dot(lo, u)
            a = jnp.where(mask, jnp.exp(z - sp + cs[h] + suffix - log1m), 0.0)
            new_cs.append(cs[h] + suffix[:, 0:1])
            new_accs.append(accs[h] + _dot(a.astype(BF16), v))
        c_max = functools.reduce(jnp.maximum, [jnp.max(c) for c in new_cs])
        return i + 1, tuple(new_cs), tuple(new_accs), c_max

    init = (jnp.int32(0), tuple(jnp.zeros((tq, 1), F32) for _ in range(nh)),
            tuple(jnp.zeros((tq, HEAD_DIM), F32) for _ in range(nh)), jnp.float32(0.0))
    _, _, accs, _ = lax.while_loop(live, step, init)
    for h in range(nh):
        o_ref[:, h * HEAD_DIM:(h + 1) * HEAD_DIM] = accs[h]


def _sb_prompt(z, b, t):
    tq = 128
    tk = 128
    nh = 4
    nq = t // tq
    gw = nh * HEAD_DIM
    u = (np.arange(tk)[:, None] >= np.arange(tk)[None, :]).astype(np.float32)
    qb, kb, vb = OFF_QB // gw, OFF_KB // gw, OFF_VB // gw
    return pl.pallas_call(
        functools.partial(_sb_body, tq=tq, tk=tk, nh=nh),
        grid=(b, SB_HEADS // nh, nq),
        in_specs=[pl.BlockSpec((tq, gw), lambda i, h, j: (i * nq + j, qb + h)),
                  pl.BlockSpec((t, gw), lambda i, h, j: (i, kb + h)),
                  pl.BlockSpec((t, gw), lambda i, h, j: (i, vb + h)),
                  pl.BlockSpec((tk, tk), lambda i, h, j: (0, 0))],
        out_specs=pl.BlockSpec((tq, gw), lambda i, h, j: (i * nq + j, h)),
        out_shape=jax.ShapeDtypeStruct((b * t, SB_W), F32),
        compiler_params=_params(("parallel", "parallel", "arbitrary"), 48),
        name="sb_prompt",
    )(z, z, z, jnp.asarray(u, BF16))


def _mix_body(oa_ref, ob_ref, za_ref, zb_ref, m0_ref, m1_ref, x_ref, wa_ref, wb_ref, wo_ref, h_ref):
    ua = (oa_ref[...] * _silu(za_ref[...])).astype(BF16)
    ub = (ob_ref[...] * _silu(zb_ref[...])).astype(BF16)
    ya = _dot(ua, wa_ref[...])
    yb = _dot(ub, wb_ref[...])
    mix = jax.nn.sigmoid(m0_ref[...]) * ya + jax.nn.sigmoid(m1_ref[...]) * yb
    h_ref[...] = x_ref[...] + _dot(mix.astype(BF16), wo_ref[...])


def _mix(o_a, o_b, z, x, w_up_a, w_up_b, w_out, l):
    n, d = x.shape
    tm = min(n, 256)
    once = pl.Buffered(1)
    return pl.pallas_call(
        _mix_body,
        grid=(n // tm,),
        in_specs=[pl.BlockSpec((tm, NSA_W), lambda i: (i, 0)),
                  pl.BlockSpec((tm, SB_W), lambda i: (i, 0)),
                  pl.BlockSpec((tm, NSA_W), lambda i: (i, OFF_ZA // NSA_W)),
                  pl.BlockSpec((tm, SB_W), lambda i: (i, OFF_ZB // SB_W)),
                  pl.BlockSpec((tm, d), lambda i: (i, OFF_M0 // d)),
                  pl.BlockSpec((tm, d), lambda i: (i, OFF_M0 // d + 1)),
                  pl.BlockSpec((tm, d), lambda i: (i, 0)),
                  pl.BlockSpec((None, NSA_W, d), lambda i: (l, 0, 0), pipeline_mode=once),
                  pl.BlockSpec((None, SB_W, d), lambda i: (l, 0, 0), pipeline_mode=once),
                  pl.BlockSpec((None, d, d), lambda i: (l, 0, 0), pipeline_mode=once)],
        out_specs=pl.BlockSpec((tm, d), lambda i: (i, 0)),
        out_shape=jax.ShapeDtypeStruct((n, d), F32),
        compiler_params=_params(("parallel",), 56),
        name="mix",
    )(o_a, o_b, z, z, z, z, x, w_up_a, w_up_b, w_out)


def _ple_body(h_ref, p_ref, g_ref, wpg_ref, wple_ref, o_ref):
    h = h_ref[...]
    hn = (h * lax.rsqrt(jnp.mean(h * h, axis=-1, keepdims=True) + RMS_EPS) * g_ref[...]).astype(BF16)
    gate = jax.nn.sigmoid(_dot(hn, wpg_ref[...]))
    o_ref[...] = h + gate * _dot(p_ref[...].astype(BF16), wple_ref[...])


def _ple(h, p, g, w_pg, w_ple, l):
    n, d = h.shape
    pd = p.shape[1]
    tm = min(n, 256)
    once = pl.Buffered(1)
    return pl.pallas_call(
        _ple_body,
        grid=(n // tm,),
        in_specs=[pl.BlockSpec((tm, d), lambda i: (i, 0)),
                  pl.BlockSpec((tm, pd), lambda i: (i, 0)),
                  pl.BlockSpec((1, d), lambda i: (0, 0)),
                  pl.BlockSpec((None, d, d), lambda i: (l, 0, 0), pipeline_mode=once),
                  pl.BlockSpec((None, pd, d), lambda i: (l, 0, 0), pipeline_mode=once)],
        out_specs=pl.BlockSpec((tm, d), lambda i: (i, 0)),
        out_shape=jax.ShapeDtypeStruct((n, d), F32),
        compiler_params=_params(("parallel",), 40),
        name="ple",
    )(h, p, g, w_pg, w_ple)


def _sbdec_body(pt_ref, qm_ref, cache_ref, tri_ref, ex_ref, o_ref, buf_ref, sem_ref, *, l, n_pages):
    i = pl.program_id(0)

    def page_copy(p, slot):
        page = pt_ref[i * n_pages + n_pages - 1 - p]
        return pltpu.make_async_copy(cache_ref.at[l, page], buf_ref.at[slot], sem_ref.at[slot])

    page_copy(0, 0).start()
    tri = tri_ref[...]
    head_lane = _iota((1, LANES), 1) < SB_HEADS

    def live(carry):
        p, _, _, c_max = carry
        return jnp.logical_and(p < n_pages, c_max > -SB_UNDERFLOW)

    def step(carry):
        p, c, acc, _ = carry
        slot = lax.rem(p, 2)
        page_copy(p, slot).wait()

        @pl.when(p + 1 < n_pages)
        def _():
            page_copy(p + 1, 1 - slot).start()

        page = buf_ref.at[slot]
        z = sum(_dot(_rows(page, h, PAGE_SIZE, SB_ROWS).astype(BF16), qm_ref[h * HEAD_DIM:(h + 1) * HEAD_DIM, :])
                for h in range(SB_HEADS))
        sp = _softplus(z)
        log1m = -sp
        hi, lo = _split2(log1m)
        suffix = _dot(tri, hi) + _dot(tri, lo)
        a = jnp.exp(z - sp + c + suffix - log1m)
        a_wide = _dot(a.astype(BF16), ex_ref[...])
        upd = jnp.concatenate(
            [jnp.sum(a_wide[:, h * HEAD_DIM:(h + 1) * HEAD_DIM] * _rows(page, SB_HEADS + h, PAGE_SIZE, SB_ROWS),
                     axis=0, keepdims=True) for h in range(SB_HEADS)], axis=1)
        c = c + suffix[0:1, :]
        c_max = jnp.max(jnp.where(head_lane, c, -jnp.inf))
        return p + 1, c, acc + upd, c_max

    init = (jnp.int32(0), jnp.zeros((1, LANES), F32), jnp.zeros((1, SB_W), F32), jnp.float32(0.0))
    p_end, _, acc, _ = lax.while_loop(live, step, init)

    @pl.when(p_end < n_pages)
    def _():
        page_copy(p_end, lax.rem(p_end, 2)).wait()

    o_ref[...] = acc


def _sb_sample(q_b, cache_sb, page_table, l):
    b, n_pages = page_table.shape
    head_of_row = np.arange(SB_W) // HEAD_DIM
    onehot = (head_of_row[:, None] == np.arange(LANES)[None, :])
    qm = jnp.where(jnp.asarray(onehot)[None], q_b[:, :, None], 0.0).astype(BF16)
    tri = (np.arange(PAGE_SIZE)[:, None] <= np.arange(PAGE_SIZE)[None, :]).astype(np.float32)
    grid_spec = pltpu.PrefetchScalarGridSpec(
        num_scalar_prefetch=1,
        grid=(b,),
        in_specs=[pl.BlockSpec((None, SB_W, LANES), lambda i, pt: (i, 0, 0)),
                  pl.BlockSpec(memory_space=pl.ANY),
                  pl.BlockSpec((PAGE_SIZE, PAGE_SIZE), lambda i, pt: (0, 0)),
                  pl.BlockSpec((LANES, SB_W), lambda i, pt: (0, 0))],
        out_specs=pl.BlockSpec((None, 1, SB_W), lambda i, pt: (i, 0, 0)),
        scratch_shapes=[pltpu.VMEM((2, PAGE_SIZE * SB_ROWS, HEAD_DIM), F32),
                        pltpu.SemaphoreType.DMA((2,))])
    out = pl.pallas_call(
        functools.partial(_sbdec_body, l=l, n_pages=n_pages),
        grid_spec=grid_spec,
        out_shape=jax.ShapeDtypeStruct((b, 1, SB_W), F32),
        compiler_params=_params(("arbitrary",), 32),
        name="sb_sample",
    )(page_table.reshape(-1), qm, cache_sb, jnp.asarray(tri, BF16), jnp.asarray(onehot.T, BF16))
    return out.reshape(b, SB_W)


def _nsadec_cmp_body(*refs, n_pc, n_cmp, n_sel, k_top, t_q):
    pt_ref, qn_ref = refs[0], refs[1]
    page_refs = refs[2:2 + n_pc]
    pos_ref, w1_ref, b1_ref, w2_ref, ov_ref, sel_ref, oc_ref, p_ref, q_ref = refs[2 + n_pc:]
    del pt_ref
    c = pl.program_id(1)
    rows_pp = PAGE_SIZE // CMP_STRIDE
    half = CMP_STRIDE * HEAD_DIM
    n_chunk = n_pc * rows_pp

    def block_rows(page_ref, kg):
        return jnp.concatenate(
            [_rows(page_ref, l * NSA_ROWS + kg, rows_pp, CMP_STRIDE * NSA_ROWS) for l in range(CMP_STRIDE)],
            axis=1)

    for kind in range(2):
        r = jnp.concatenate([block_rows(page_refs[i], kind * NSA_KV_GROUPS + g)
                             for g in range(NSA_KV_GROUPS) for i in range(n_pc)], axis=0)
        p = _dot((r + pos_ref[kind, 0:1, :]).astype(BF16), w1_ref[kind, 0:half, :])
        q = _dot((r + pos_ref[kind, 1:2, :]).astype(BF16), w1_ref[kind, half:2 * half, :])
        r0 = pl.multiple_of(c * n_chunk, n_chunk)
        for g in range(NSA_KV_GROUPS):
            p_ref[kind, g, pl.ds(r0, n_chunk), :] = p[g * n_chunk:(g + 1) * n_chunk]
            q_ref[kind, g, pl.ds(r0, n_chunk), :] = q[g * n_chunk:(g + 1) * n_chunk]

    @pl.when(c == pl.num_programs(1) - 1)
    def _():
        nc = p_ref.shape[2]
        nsp = ov_ref.shape[1]
        qn = qn_ref[...]
        head_group = _iota((NSA_HEADS, 1), 0) // HEADS_PER_GROUP
        n_idx = _iota((1, nc), 1)
        cmask = (n_idx * CMP_STRIDE + (CMP_BLOCK - 1) <= t_q) & (n_idx < n_cmp)
        blk = _iota((1, nsp), 1)
        o_c = jnp.zeros((NSA_HEADS, HEAD_DIM), F32)
        sel_rows = jnp.zeros((NSA_HEADS, nsp), F32)
        for g in range(NSA_KV_GROUPS):
            ckv = []
            for kind in range(2):
                hid = _cmp_hidden(p_ref[kind, g], q_ref[kind, g], b1_ref[kind])
                ckv.append(_dot(hid.astype(BF16), w2_ref[kind]).astype(BF16))
            p = _masked_softmax(_dot_nt(qn, ckv[0]), cmask)
            mine = head_group == g
            o_c = jnp.where(mine, _dot(p.astype(BF16), ckv[1]), o_c)
            psum = jnp.sum(jnp.where(mine, p, 0.0), axis=0, keepdims=True)
            psum = jnp.broadcast_to(psum, (SUBLANES, nc))
            imp = sum(_dot(part, ov_ref[...]) for part in _split3(psum))[0:1, :]
            score = _block_scores(imp, blk, t_q, n_sel)
            along_lanes = jnp.broadcast_to(score, (nsp, nsp))
            along_rows = along_lanes.T
            beats = jnp.where(_iota((nsp, nsp), 0) < _iota((nsp, nsp), 1),
                              jnp.where(along_rows >= along_lanes, 1.0, 0.0),
                              jnp.where(along_rows > along_lanes, 1.0, 0.0))
            rank = jnp.sum(beats, axis=0, keepdims=True)
            chosen = jnp.where(rank < k_top, 1.0, 0.0)
            sel_rows = jnp.where(_iota((NSA_HEADS, 1), 0) == g, chosen, sel_rows)
        sel_ref[...] = sel_rows
        oc_ref[...] = o_c


def _nsa_sample_select(qn, cache_nsa, page_table, pos, w1, b1, w2, l, past_len):
    b, n_pages = page_table.shape
    n_pc = min(8, n_pages)
    t_all = past_len + 1
    nc = past_len // CMP_STRIDE
    ns_pad = -(-(-(-t_all // SEL_BLOCK)) // LANES) * LANES
    n_cmp, n_sel, ovt = _sel_geometry(t_all, nc, ns_pad)
    k_top = min(N_SELECT, n_sel)
    half = CMP_STRIDE * HEAD_DIM
    kernel = functools.partial(_nsadec_cmp_body, n_pc=n_pc, n_cmp=n_cmp, n_sel=n_sel, k_top=k_top,
                               t_q=past_len)

    def page_spec(i_pc):
        return pl.BlockSpec((None, None, PAGE_SIZE * NSA_ROWS, HEAD_DIM),
                            lambda i, c, pt: (l, pt[i * n_pages + c * n_pc + i_pc], 0, 0))

    grid_spec = pltpu.PrefetchScalarGridSpec(
        num_scalar_prefetch=1,
        grid=(b, n_pages // n_pc),
        in_specs=[pl.BlockSpec((None, NSA_HEADS, HEAD_DIM), lambda i, c, pt: (i, 0, 0))]
        + [page_spec(i_pc) for i_pc in range(n_pc)]
        + [pl.BlockSpec((None, 2, 2, half), lambda i, c, pt: (l, 0, 0, 0)),
           pl.BlockSpec((None, 2, 2 * half, HEAD_DIM), lambda i, c, pt: (l, 0, 0, 0)),
           pl.BlockSpec((None, 2, 1, HEAD_DIM), lambda i, c, pt: (l, 0, 0, 0)),
           pl.BlockSpec((None, 2, HEAD_DIM, HEAD_DIM), lambda i, c, pt: (l, 0, 0, 0)),
           pl.BlockSpec((nc, ns_pad), lambda i, c, pt: (0, 0))],
        out_specs=[pl.BlockSpec((None, NSA_HEADS, ns_pad), lambda i, c, pt: (i, 0, 0)),
                   pl.BlockSpec((None, NSA_HEADS, HEAD_DIM), lambda i, c, pt: (i, 0, 0))],
        scratch_shapes=[pltpu.VMEM((2, NSA_KV_GROUPS, nc, HEAD_DIM), F32),
                        pltpu.VMEM((2, NSA_KV_GROUPS, nc, HEAD_DIM), F32)])
    sel, o_c = pl.pallas_call(
        kernel,
        grid_spec=grid_spec,
        out_shape=[jax.ShapeDtypeStruct((b, NSA_HEADS, ns_pad), F32),
                   jax.ShapeDtypeStruct((b, NSA_HEADS, HEAD_DIM), F32)],
        compiler_params=_params(("parallel", "arbitrary"), 48),
        name="nsa_sample_select",
    )(page_table.reshape(-1), qn, *([cache_nsa] * n_pc), pos, w1, b1, w2,
      jnp.asarray(ovt.T, BF16))
    return sel, o_c, n_sel, k_top


def _nsadec_att_body(*refs, k_top, n_sel):
    idx_ref, pt_ref, qr_ref, gate_ref, oc_ref = refs[:5]
    blk_refs = refs[5:5 + k_top]
    new_ref, wnew_ref, win_ref, o_ref = refs[5 + k_top:]
    del pt_ref
    i = pl.program_id(0)
    g = pl.program_id(1)
    q = qr_ref[...]
    qf = q.astype(F32)
    base = (i * NSA_KV_GROUPS + g) * k_top
    g_n = NSA_KV_GROUPS

    def of_group(ref, kind, count, stride):
        if count == 1:
            return jnp.where(g == 0, ref[kind * g_n:kind * g_n + 1, :], ref[kind * g_n + 1:kind * g_n + 2, :])
        return jnp.where(g == 0, _rows(ref, kind * g_n, count, stride), _rows(ref, kind * g_n + 1, count, stride))

    k_all = jnp.concatenate([of_group(r, 2, SEL_BLOCK, NSA_ROWS) for r in blk_refs], axis=0).astype(BF16)
    v_all = jnp.concatenate([of_group(r, 3, SEL_BLOCK, NSA_ROWS) for r in blk_refs], axis=0).astype(BF16)
    slot = _iota((1, k_top * SEL_BLOCK), 1) >> (SEL_BLOCK.bit_length() - 1)
    cached = jnp.zeros((1, k_top * SEL_BLOCK), F32)
    n_new = jnp.int32(0)
    for s in range(k_top):
        is_new = idx_ref[base + s] == n_sel - 1
        cached = jnp.where(slot == s, jnp.where(is_new, 0.0, 1.0), cached)
        n_new = n_new + is_new.astype(jnp.int32)
    valid = cached > 0.5
    has_new = n_new > 0
    sc = jnp.where(valid, _dot_nt(q, k_all), NEG_BIG)
    s_new = jnp.sum(qf * of_group(new_ref, 2, 1, 1), axis=-1, keepdims=True)
    s_new = jnp.where(has_new, s_new, NEG_BIG)
    m_s = jnp.maximum(jnp.max(sc, axis=-1, keepdims=True), s_new)
    p = jnp.where(valid, jnp.exp(sc - m_s), 0.0)
    p_new = jnp.where(has_new, jnp.exp(s_new - m_s), 0.0)
    l_s = jnp.sum(p, axis=-1, keepdims=True) + p_new
    o_s = (_dot(p.astype(BF16), v_all) + p_new * of_group(new_ref, 3, 1, 1)) / jnp.where(l_s > 0.0, l_s, 1.0)

    w_buf = win_ref.shape[0] // WIN_ROWS
    sw = _dot_nt(q, of_group(win_ref, 0, w_buf, WIN_ROWS).astype(BF16))
    wmask = (w_buf - _iota((1, w_buf), 1)) < WINDOW
    sw = jnp.where(wmask, sw, NEG_BIG)
    sw_new = jnp.sum(qf * of_group(wnew_ref, 0, 1, 1), axis=-1, keepdims=True)
    mw = jnp.maximum(jnp.max(sw, axis=-1, keepdims=True), sw_new)
    pw = jnp.where(wmask, jnp.exp(sw - mw), 0.0)
    pw_new = jnp.exp(sw_new - mw)
    l_w = jnp.sum(pw, axis=-1, keepdims=True) + pw_new
    o_w = (_dot(pw.astype(BF16), of_group(win_ref, 1, w_buf, WIN_ROWS).astype(BF16))
           + pw_new * of_group(wnew_ref, 1, 1, 1)) / l_w

    gates = jax.nn.sigmoid(gate_ref[...])
    o_ref[...] = gates[0] * oc_ref[...] + gates[1] * o_s + gates[2] * o_w


def _nsa_sample_attend(idx, qr, gates, o_c, cache_nsa, page_table, nsa_new, win_new, state_win, l,
                       n_sel, k_top):
    b, n_pages = page_table.shape
    per_page = PAGE_SIZE // SEL_BLOCK
    last = n_sel - 2
    g_n = NSA_KV_GROUPS
    blk_rows = SEL_BLOCK * NSA_ROWS
    win_rows = state_win.shape[2]

    def blk_map(i, g, idx_ref, pt, *, s):
        blk = jnp.minimum(idx_ref[(i * g_n + g) * k_top + s], last)
        return (l, pt[i * n_pages + blk // per_page], blk % per_page, 0, 0)

    grid_spec = pltpu.PrefetchScalarGridSpec(
        num_scalar_prefetch=2,
        grid=(b, g_n),
        in_specs=[pl.BlockSpec((None, NSA_HEADS, HEAD_DIM), lambda i, g, *_: (i, 0, 0)),
                  pl.BlockSpec((None, 3, NSA_HEADS, HEAD_DIM), lambda i, g, *_: (i, 0, 0, 0)),
                  pl.BlockSpec((None, NSA_HEADS, HEAD_DIM), lambda i, g, *_: (i, 0, 0))]
        + [pl.BlockSpec((None, None, None, blk_rows, HEAD_DIM), functools.partial(blk_map, s=s))
           for s in range(k_top)]
        + [pl.BlockSpec((None, NSA_ROWS, HEAD_DIM), lambda i, g, *_: (i, 0, 0)),
           pl.BlockSpec((None, WIN_ROWS, HEAD_DIM), lambda i, g, *_: (i, 0, 0)),
           pl.BlockSpec((None, None, win_rows, HEAD_DIM), lambda i, g, *_: (l, i, 0, 0))],
        out_specs=pl.BlockSpec((None, None, NSA_HEADS, HEAD_DIM), lambda i, g, *_: (i, g, 0, 0)))
    return pl.pallas_call(
        functools.partial(_nsadec_att_body, k_top=k_top, n_sel=n_sel),
        grid_spec=grid_spec,
        out_shape=jax.ShapeDtypeStruct((b, g_n, NSA_HEADS, HEAD_DIM), F32),
        compiler_params=_params(("parallel", "parallel"), 32),
        name="nsa_sample_attend",
    )(idx.reshape(-1), page_table.reshape(-1), qr, gates, o_c, *([cache_nsa] * k_top), nsa_new, win_new,
      state_win)


def _permute_w_in(w_in, d_model):
    o = np.cumsum([0, NSA_W, 6 * KV_W, 3 * NSA_HEADS, NSA_W, 3 * SB_W, SB_W, 2 * d_model])
    depth = w_in.shape[0]
    gate = w_in[:, :, o[2]:o[3]]
    per_group = 3 * HEADS_PER_GROUP
    gate_cols = []
    for g in range(NSA_KV_GROUPS):
        gate_cols.append(gate[:, :, g * per_group:(g + 1) * per_group])
        gate_cols.append(jnp.zeros((depth, d_model, LANES - per_group), w_in.dtype))
    gate_cols.append(jnp.zeros((depth, d_model, GA_W - NSA_KV_GROUPS * LANES), w_in.dtype))
    cols = [w_in[:, :, o[0]:o[1]], w_in[:, :, o[3]:o[4]], w_in[:, :, o[5]:o[6]],
            w_in[:, :, o[4]:o[4] + SB_W] * SCALE, w_in[:, :, o[4] + SB_W:o[5]],
            w_in[:, :, o[6]:o[7]], w_in[:, :, o[1]:o[2]]] + gate_cols
    return jnp.concatenate([c.astype(BF16) for c in cols], axis=-1)


def kernel(x_prompt, x_sample, cache_nsa, cache_sb, state_win, page_table, p_prompt, p_sample, g_mix,
           w_in, g_q, g_k, cmp_pos, w_c1, b_c1, w_c2, w_up_a, w_up_b, w_out, g_ple, w_pg, w_ple):
    depth = w_in.shape[0]
    bp, t, d = x_prompt.shape
    bs, ts, _ = x_sample.shape
    assert ts == 1
    n_pages = page_table.shape[1]
    past_len = n_pages * PAGE_SIZE
    n_pool = cache_nsa.shape[1]
    w_buf = state_win.shape[2]
    off_m1, off_kv, off_ga, n_cols = _offsets(d)
    half = CMP_STRIDE * HEAD_DIM

    w_proj = _permute_w_in(w_in, d)
    wa, wb, wo = w_up_a.astype(BF16), w_up_b.astype(BF16), w_out.astype(BF16)
    wpg, wple = w_pg.astype(BF16), w_ple.astype(BF16)
    pos2 = cmp_pos.reshape(depth, 2, 2, half)
    w1 = w_c1.reshape(depth, 2, 2 * half, HEAD_DIM).astype(BF16)
    b1 = b_c1.reshape(depth, 2, 1, HEAD_DIM)
    w2 = w_c2.astype(BF16)
    cos_p, sin_p = _rope_tables(jnp.arange(t))
    cos_s, sin_s = _rope_tables(jnp.full((bs,), past_len))
    nsa_pages = cache_nsa.reshape(depth, n_pool, PAGE_SIZE * NSA_ROWS, HEAD_DIM)
    nsa_blocks = cache_nsa.reshape(depth, n_pool, PAGE_SIZE // SEL_BLOCK, SEL_BLOCK * NSA_ROWS, HEAD_DIM)
    sb_pages = cache_sb.reshape(depth, n_pool, PAGE_SIZE * SB_ROWS, HEAD_DIM)
    win_state = state_win.reshape(depth, bs, w_buf * WIN_ROWS, HEAD_DIM)

    hp = x_prompt.reshape(bp * t, d)
    hs = x_sample.reshape(bs, d)
    outs = [[] for _ in range(6)]
    for l in range(depth):
        gm, gq, gp = g_mix[l][None], g_q[l][None], g_ple[l][None]

        z = _proj(hp, gm, w_proj, l)
        qn, qr, kv16, cmp_in, nsa_rows, win_rows = _prep(z, cos_p, sin_p, gq, g_k[l], off_kv)
        ckv = _compress(cmp_in, bp, t, pos2, w1, b1, w2, l)
        o_a = _nsa_prompt(qn, qr, z, ckv, kv16, bp, t, off_ga)
        o_b = _sb_prompt(z, bp, t)
        h1 = _mix(o_a, o_b, z, hp, wa, wb, wo, l)
        hp = _ple(h1, p_prompt[l].reshape(bp * t, -1), gp, wpg, wple, l)
        outs[0].append(nsa_rows.reshape(bp, t, 4, NSA_KV_GROUPS, HEAD_DIM))
        outs[2].append(z[:, OFF_KB:OFF_KB + 2 * SB_W].reshape(bp, t, 2, SB_HEADS, HEAD_DIM))
        keep = min(WINDOW, t)
        outs[4].append(win_rows.reshape(bp, t, 2, NSA_KV_GROUPS, HEAD_DIM)[:, t - keep:])

        zs = _proj(hs, gm, w_proj, l)
        qn_s, qr_s, _, _, nsa_new, win_new = _prep(zs, cos_s, sin_s, gq, g_k[l], off_kv)
        sel, o_c, n_sel, k_top = _nsa_sample_select(
            qn_s.reshape(bs, NSA_HEADS, HEAD_DIM), nsa_pages, page_table, pos2, w1, b1, w2, l, past_len)
        _, idx = lax.top_k(sel[:, :NSA_KV_GROUPS, :], k_top)
        gate_logits = zs[:, off_ga:off_ga + NSA_KV_GROUPS * LANES].reshape(bs, NSA_KV_GROUPS, LANES)
        gate_logits = gate_logits[:, :, :3 * HEADS_PER_GROUP].reshape(bs, NSA_HEADS, 3)
        gate_logits = jnp.broadcast_to(gate_logits.transpose(0, 2, 1)[..., None], (bs, 3, NSA_HEADS, HEAD_DIM))
        o_full = _nsa_sample_attend(idx.astype(jnp.int32), qr_s.reshape(bs, NSA_HEADS, HEAD_DIM), gate_logits,
                                    o_c, nsa_blocks, page_table, nsa_new.reshape(bs, NSA_ROWS, HEAD_DIM),
                                    win_new.reshape(bs, WIN_ROWS, HEAD_DIM), win_state, l, n_sel, k_top)
        o_a_s = jnp.concatenate(
            [o_full[:, g, g * HEADS_PER_GROUP:(g + 1) * HEADS_PER_GROUP] for g in range(NSA_KV_GROUPS)],
            axis=1).reshape(bs, NSA_W)
        o_b_s = _sb_sample(zs[:, OFF_QB:OFF_QB + SB_W], sb_pages, page_table, l)
        h1s = _mix(o_a_s, o_b_s, zs, hs, wa, wb, wo, l)
        hs = _ple(h1s, p_sample[l].reshape(bs, -1), gp, wpg, wple, l)
        outs[1].append(nsa_new.reshape(bs, 1, 4, NSA_KV_GROUPS, HEAD_DIM))
        outs[3].append(zs[:, OFF_KB:OFF_KB + 2 * SB_W].reshape(bs, 1, 2, SB_HEADS, HEAD_DIM))
        new_win = jnp.concatenate([win_state[l], win_new.reshape(bs, WIN_ROWS, HEAD_DIM)], axis=1)[:, WIN_ROWS:]
        outs[5].append(new_win.reshape(bs, w_buf, 2, NSA_KV_GROUPS, HEAD_DIM))

    return (hp.reshape(bp, t, d), hs.reshape(bs, 1, d), jnp.stack(outs[0]), jnp.stack(outs[1]),
            jnp.stack(outs[2]), jnp.stack(outs[3]), jnp.stack(outs[4]), jnp.stack(outs[5]))
```

```python
import functools

import numpy as np
import jax
import jax.numpy as jnp
from jax import lax
from jax.experimental import pallas as pl
from jax.experimental.pallas import tpu as pltpu

HEAD_DIM = 128
NSA_HEADS = 8
NSA_KV_GROUPS = 2
HEADS_PER_GROUP = NSA_HEADS // NSA_KV_GROUPS
SB_HEADS = 8
CMP_BLOCK = 32
CMP_STRIDE = 16
SEL_BLOCK = 64
N_SELECT = 16
WINDOW = 512
ROPE_THETA = 10000.0
PAGE_SIZE = 128
RMS_EPS = 1e-6
SEL_BONUS = 1e3
NEG_BIG = -1e30
BELOW_NEG_BIG = -3e38
SB_UNDERFLOW = 104.0

NSA_W = NSA_HEADS * HEAD_DIM
KV_W = NSA_KV_GROUPS * HEAD_DIM
SB_W = SB_HEADS * HEAD_DIM
NSA_ROWS = 4 * NSA_KV_GROUPS
WIN_ROWS = 2 * NSA_KV_GROUPS
SB_ROWS = 2 * SB_HEADS
LANES = 128
SUBLANES = 8
SCALE = HEAD_DIM ** -0.5
V7X_VMEM_BYTES = 64 * 1024 * 1024

F32 = jnp.float32
BF16 = jnp.bfloat16

OFF_QA = 0
OFF_ZA = NSA_W
OFF_ZB = OFF_ZA + NSA_W
OFF_QB = OFF_ZB + SB_W
OFF_KB = OFF_QB + SB_W
OFF_VB = OFF_KB + SB_W
OFF_M0 = OFF_VB + SB_W
GA_W = 512


def _offsets(d_model):
    off_m1 = OFF_M0 + d_model
    off_kv = off_m1 + d_model
    off_ga = off_kv + 6 * KV_W
    return off_m1, off_kv, off_ga, off_ga + GA_W


def _params(sem, vmem_mb):
    return pltpu.CompilerParams(dimension_semantics=sem,
                                vmem_limit_bytes=min(vmem_mb * 1024 * 1024, V7X_VMEM_BYTES - (4 << 20)))


def _dot(a, b):
    return jnp.dot(a, b, preferred_element_type=F32)


def _dot_nt(a, b):
    return lax.dot_general(a, b, (((1,), (1,)), ((), ())), preferred_element_type=F32)


def _split2(a):
    hi = a.astype(BF16)
    return hi, (a - hi.astype(F32)).astype(BF16)


def _split3(a):
    hi = a.astype(BF16)
    r1 = a - hi.astype(F32)
    mid = r1.astype(BF16)
    lo = (r1 - mid.astype(F32)).astype(BF16)
    return hi, mid, lo


def _iota(shape, dim):
    return lax.broadcasted_iota(jnp.int32, shape, dim)


def _silu(x):
    return x * jax.nn.sigmoid(x)


def _softplus(z):
    return jnp.maximum(z, 0.0) + jnp.log1p(jnp.exp(-jnp.abs(z)))


def _masked_softmax(s, mask):
    s = jnp.where(mask, s, NEG_BIG)
    m = jnp.max(s, axis=-1, keepdims=True)
    p = jnp.where(mask, jnp.exp(s - m), 0.0)
    l = jnp.sum(p, axis=-1, keepdims=True)
    return p / jnp.where(l > 0.0, l, 1.0)


def _rows(ref, start, count, stride):
    return ref[pl.ds(start, count, stride=stride), :]


def _proj_body(x_ref, g_ref, w_ref, o_ref, xn_ref):
    @pl.when(pl.program_id(1) == 0)
    def _():
        x = x_ref[...]
        ms = jnp.mean(x * x, axis=-1, keepdims=True)
        xn_ref[...] = (x * lax.rsqrt(ms + RMS_EPS) * g_ref[...]).astype(BF16)

    o_ref[...] = _dot(xn_ref[...], w_ref[...])


def _proj(x, g, w, l):
    n, d = x.shape
    c = w.shape[2]
    tm = min(n, 1024)
    tn = 1024 if c % 1024 == 0 else 512
    return pl.pallas_call(
        _proj_body,
        grid=(n // tm, c // tn),
        in_specs=[pl.BlockSpec((tm, d), lambda i, j: (i, 0)),
                  pl.BlockSpec((1, d), lambda i, j: (0, 0)),
                  pl.BlockSpec((None, d, tn), lambda i, j: (l, 0, j))],
        out_specs=pl.BlockSpec((tm, tn), lambda i, j: (i, j)),
        out_shape=jax.ShapeDtypeStruct((n, c), F32),
        scratch_shapes=[pltpu.VMEM((tm, d), BF16)],
        compiler_params=_params(("parallel", "arbitrary"), 48),
        name="proj",
    )(x, g, w)


def _prep_body(qa_ref, kc_ref, ks_ref, kw_ref, cos_ref, sin_ref, gq_ref, gk_ref,
               qn_ref, qr_ref, kv_ref, cmp_ref, nsa_ref, win_ref, *, tm):
    cos = cos_ref[...]
    sin = sin_ref[...]

    def norm(v, g):
        return v * lax.rsqrt(jnp.mean(v * v, axis=-1, keepdims=True) + RMS_EPS) * g

    def rope(v):
        return v * cos + pltpu.roll(v, HEAD_DIM // 2, 1) * sin

    gq = gq_ref[...]
    for h in range(NSA_HEADS):
        sl = slice(h * HEAD_DIM, (h + 1) * HEAD_DIM)
        qn = norm(qa_ref[:, sl], gq)
        qn_ref[:, sl] = (qn * SCALE).astype(BF16)
        qr_ref[:, sl] = (rope(qn) * SCALE).astype(BF16)
    for g in range(NSA_KV_GROUPS):
        k_sl = slice(g * HEAD_DIM, (g + 1) * HEAD_DIM)
        v_sl = slice(KV_W + g * HEAD_DIM, KV_W + (g + 1) * HEAD_DIM)
        rows = [norm(kc_ref[:, k_sl], gk_ref[0:1, :]), kc_ref[:, v_sl],
                rope(norm(ks_ref[:, k_sl], gk_ref[1:2, :])), ks_ref[:, v_sl],
                rope(norm(kw_ref[:, k_sl], gk_ref[2:3, :])), kw_ref[:, v_sl]]
        for kind, val in enumerate(rows):
            kv_ref[:, kind * KV_W + g * HEAD_DIM:kind * KV_W + (g + 1) * HEAD_DIM] = val.astype(BF16)
        cmp_ref[:, k_sl] = rows[0]
        cmp_ref[:, v_sl] = rows[1]
        for kind in range(4):
            nsa_ref[pl.ds(kind * NSA_KV_GROUPS + g, tm, stride=NSA_ROWS), :] = rows[kind]
        for kind in range(2):
            win_ref[pl.ds(kind * NSA_KV_GROUPS + g, tm, stride=WIN_ROWS), :] = rows[4 + kind]


def _prep(z, cos, sin, gq, gk, off_kv):
    n = z.shape[0]
    tm = min(n, 256)
    nt = cos.shape[0] // tm
    kvb = off_kv // (2 * KV_W)
    return pl.pallas_call(
        functools.partial(_prep_body, tm=tm),
        grid=(n // tm,),
        in_specs=[pl.BlockSpec((tm, NSA_W), lambda i: (i, OFF_QA // NSA_W)),
                  pl.BlockSpec((tm, 2 * KV_W), lambda i: (i, kvb)),
                  pl.BlockSpec((tm, 2 * KV_W), lambda i: (i, kvb + 1)),
                  pl.BlockSpec((tm, 2 * KV_W), lambda i: (i, kvb + 2)),
                  pl.BlockSpec((tm, HEAD_DIM), lambda i: (i % nt, 0)),
                  pl.BlockSpec((tm, HEAD_DIM), lambda i: (i % nt, 0)),
                  pl.BlockSpec((1, HEAD_DIM), lambda i: (0, 0)),
                  pl.BlockSpec((3, HEAD_DIM), lambda i: (0, 0))],
        out_specs=[pl.BlockSpec((tm, NSA_W), lambda i: (i, 0)),
                   pl.BlockSpec((tm, NSA_W), lambda i: (i, 0)),
                   pl.BlockSpec((tm, 6 * KV_W), lambda i: (i, 0)),
                   pl.BlockSpec((tm, 2 * KV_W), lambda i: (i, 0)),
                   pl.BlockSpec((tm * NSA_ROWS, HEAD_DIM), lambda i: (i, 0)),
                   pl.BlockSpec((tm * WIN_ROWS, HEAD_DIM), lambda i: (i, 0))],
        out_shape=[jax.ShapeDtypeStruct((n, NSA_W), BF16),
                   jax.ShapeDtypeStruct((n, NSA_W), BF16),
                   jax.ShapeDtypeStruct((n, 6 * KV_W), BF16),
                   jax.ShapeDtypeStruct((n, 2 * KV_W), F32),
                   jax.ShapeDtypeStruct((n * NSA_ROWS, HEAD_DIM), F32),
                   jax.ShapeDtypeStruct((n * WIN_ROWS, HEAD_DIM), F32)],
        compiler_params=_params(("parallel",), 32),
        name="prep",
    )(z, z, z, z, cos, sin, gq, gk)


def _rope_tables(pos):
    half = HEAD_DIM // 2
    inv = ROPE_THETA ** (-jnp.arange(half, dtype=F32) / half)
    ang = pos.astype(F32)[:, None] * inv[None, :]
    cos = jnp.cos(ang)
    sin = jnp.sin(ang)
    return jnp.concatenate([cos, cos], axis=-1), jnp.concatenate([-sin, sin], axis=-1)


def _cmp_hidden(p, q, b1):
    return _silu(p + pltpu.roll(q, q.shape[0] - 1, 0) + b1)


def _compress_body(rows_ref, pos_ref, w1_ref, b1_ref, w2_ref, o_ref, *, nr):
    r = jnp.concatenate([_rows(rows_ref, l, nr, CMP_STRIDE) for l in range(CMP_STRIDE)], axis=1)
    half = CMP_STRIDE * HEAD_DIM
    p = _dot((r + pos_ref[0:1, :]).astype(BF16), w1_ref[0:half, :])
    q = _dot((r + pos_ref[1:2, :]).astype(BF16), w1_ref[half:2 * half, :])
    hid = _cmp_hidden(p, q, b1_ref[...])
    o_ref[...] = _dot(hid.astype(BF16), w2_ref[...])


def _compress(cmp_in, b, t, pos, w1, b1, w2, l):
    nr = t // CMP_STRIDE
    half = CMP_STRIDE * HEAD_DIM
    return pl.pallas_call(
        functools.partial(_compress_body, nr=nr),
        grid=(b, NSA_KV_GROUPS, 2),
        in_specs=[pl.BlockSpec((t, HEAD_DIM), lambda i, g, k: (i, k * NSA_KV_GROUPS + g)),
                  pl.BlockSpec((None, None, 2, half), lambda i, g, k: (l, k, 0, 0)),
                  pl.BlockSpec((None, None, 2 * half, HEAD_DIM), lambda i, g, k: (l, k, 0, 0)),
                  pl.BlockSpec((None, None, 1, HEAD_DIM), lambda i, g, k: (l, k, 0, 0)),
                  pl.BlockSpec((None, None, HEAD_DIM, HEAD_DIM), lambda i, g, k: (l, k, 0, 0))],
        out_specs=pl.BlockSpec((None, None, None, nr, HEAD_DIM), lambda i, g, k: (i, g, k, 0, 0)),
        out_shape=jax.ShapeDtypeStruct((b, NSA_KV_GROUPS, 2, nr, HEAD_DIM), F32),
        compiler_params=_params(("parallel", "parallel", "parallel"), 32),
        name="compress",
    )(cmp_in, pos, w1, b1, w2)


def _block_scores(imp, blk, t, n_sel):
    tb = t >> (SEL_BLOCK.bit_length() - 1)
    forced = (blk == 0) | (blk == tb) | (blk == tb - 1)
    future = blk * SEL_BLOCK > t
    score = jnp.where(future, NEG_BIG, imp + SEL_BONUS * jnp.where(forced, 1.0, 0.0))
    return jnp.where(blk < n_sel, score, BELOW_NEG_BIG)


def _nsa_body(qn_ref, qr_ref, ga_ref, ck_ref, cv_ref, ks_ref, vs_ref, kw_ref, vw_ref, ovt_ref, et_ref,
              o_ref, sc_ref, *, tq, tk, n_cmp, n_sel, k_top, wlen):
    j = pl.program_id(2)
    t0 = j * tq
    t_col = t0 + _iota((tq, 1), 0)
    t_row = t0 + _iota((1, tq), 1)
    nc = ck_ref.shape[0]
    nsr = ovt_ref.shape[0]
    nsp = et_ref.shape[1]

    ck = ck_ref[...].astype(BF16)
    cv = cv_ref[...].astype(BF16)
    n_idx = _iota((1, nc), 1)
    cmask = (n_idx * CMP_STRIDE + (CMP_BLOCK - 1) <= t_col) & (n_idx < n_cmp)
    nh = HEADS_PER_GROUP

    def stack(ref):
        return jnp.concatenate([ref[:, h * HEAD_DIM:(h + 1) * HEAD_DIM] for h in range(nh)], axis=0)

    def per_head(x):
        return jnp.concatenate([x] * nh, axis=0)

    p = _masked_softmax(_dot_nt(stack(qn_ref), ck), per_head(cmask))
    o_c = _dot(p.astype(BF16), cv)
    psum = sum(p[h * tq:(h + 1) * tq] for h in range(nh))

    ovt = ovt_ref[...]
    imp_t = sum(_dot_nt(ovt, part) for part in _split3(psum))
    blk = _iota((nsr, 1), 0)
    score = _block_scores(imp_t, blk, t_row, n_sel)
    sc_ref[...] = score

    def rank_step(sp, rank):
        row = sc_ref[pl.ds(sp, 1), :]
        return rank + jnp.where(sp < blk, jnp.where(row >= score, 1.0, 0.0), jnp.where(row > score, 1.0, 0.0))

    rank = lax.fori_loop(0, n_sel, rank_step, jnp.zeros((nsr, tq), F32), unroll=4 if n_sel % 4 == 0 else 1)
    chosen_t = jnp.where(rank < k_top, 1.0, 0.0)
    if nsp > nsr:
        chosen_t = jnp.concatenate([chosen_t, jnp.zeros((nsp - nsr, tq), F32)], axis=0)
    sel = chosen_t.T.astype(BF16)

    qr = stack(qr_ref)

    n_kt = (t0 + tq + tk - 1) // tk

    def scores(kb):
        k0 = pl.multiple_of(kb * tk, tk)
        chosen = _dot_nt(sel, et_ref[pl.ds(k0, tk), :])
        bias = jnp.where((chosen > 0.5) & (k0 + _iota((1, tk), 1) <= t_col), 0.0, NEG_BIG)
        return _dot_nt(qr, ks_ref[pl.ds(k0, tk), :]) + per_head(bias)

    def sel_step(kb, carry):
        s, m, l, acc = carry
        s_next = scores(jnp.minimum(kb + 1, n_kt - 1))
        v = vs_ref[pl.ds(pl.multiple_of(kb * tk, tk), tk), :]
        m_new = jnp.maximum(m, jnp.max(s, axis=-1, keepdims=True))
        p = jnp.exp(s - m_new)
        alpha = jnp.exp(m - m_new)
        return (s_next, m_new, alpha * l + jnp.sum(p, axis=-1, keepdims=True),
                alpha * acc + _dot(p.astype(BF16), v))

    init = (scores(0), jnp.full((nh * tq, 1), NEG_BIG, F32), jnp.zeros((nh * tq, 1), F32),
            jnp.zeros((nh * tq, HEAD_DIM), F32))
    _, _, l_s, acc_s = lax.fori_loop(0, n_kt, sel_step, init)
    o_s = acc_s / jnp.where(l_s > 0.0, l_s, 1.0)

    w0 = pl.multiple_of(jnp.maximum(t0 + tq - wlen, 0), tq)
    kw = kw_ref[pl.ds(w0, wlen), :]
    vw = vw_ref[pl.ds(w0, wlen), :]
    dt = t_col - (w0 + _iota((1, wlen), 1))
    wbias = jnp.where((dt >= 0) & (dt < WINDOW), 0.0, NEG_BIG)

    s = _dot_nt(qr, kw) + per_head(wbias)
    p = jnp.exp(s - jnp.max(s, axis=-1, keepdims=True))
    o_w = _dot(p.astype(BF16), vw) / jnp.sum(p, axis=-1, keepdims=True)

    gates = jax.nn.sigmoid(ga_ref[...])
    for h in range(nh):
        rows = slice(h * tq, (h + 1) * tq)
        g_c = gates[:, 3 * h:3 * h + 1]
        g_s = gates[:, 3 * h + 1:3 * h + 2]
        g_w = gates[:, 3 * h + 2:3 * h + 3]
        o_ref[:, h * HEAD_DIM:(h + 1) * HEAD_DIM] = g_c * o_c[rows] + g_s * o_s[rows] + g_w * o_w[rows]


def _sel_geometry(t_all, nc_pad, ns_pad):
    n_cmp = (t_all - CMP_BLOCK) // CMP_STRIDE + 1
    n_sel = -(-t_all // SEL_BLOCK)
    cmp_start = np.arange(n_cmp) * CMP_STRIDE
    sel_start = np.arange(n_sel) * SEL_BLOCK
    overlap = ((cmp_start[:, None] < sel_start[None, :] + SEL_BLOCK)
               & (cmp_start[:, None] + CMP_BLOCK > sel_start[None, :])).astype(np.float32)
    ovt = np.zeros((ns_pad, nc_pad), np.float32)
    ovt[:n_sel, :n_cmp] = overlap.T
    return n_cmp, n_sel, ovt


def _nsa_prompt(qn, qr, z, ckv, kv16, b, t, off_ga):
    tq = 128
    tk = min(512, t)
    nq = t // tq
    nc = t // CMP_STRIDE
    n_sel = -(-t // SEL_BLOCK)
    ns_rows = -(-n_sel // SUBLANES) * SUBLANES
    ns_pad = -(-n_sel // LANES) * LANES
    n_cmp, n_sel, ovt = _sel_geometry(t, nc, ns_rows)
    k_top = min(N_SELECT, n_sel)
    wlen = min(WINDOW + tq, t)
    et = (np.arange(t)[:, None] // SEL_BLOCK == np.arange(ns_pad)[None, :]).astype(np.float32)
    gab = off_ga // LANES
    kernel = functools.partial(_nsa_body, tq=tq, tk=tk, n_cmp=n_cmp, n_sel=n_sel, k_top=k_top, wlen=wlen)
    gw = HEADS_PER_GROUP * HEAD_DIM
    g_n = NSA_KV_GROUPS
    return pl.pallas_call(
        kernel,
        grid=(b, g_n, nq),
        in_specs=[pl.BlockSpec((tq, gw), lambda i, g, j: (i * nq + j, g)),
                  pl.BlockSpec((tq, gw), lambda i, g, j: (i * nq + j, g)),
                  pl.BlockSpec((tq, LANES), lambda i, g, j: (i * nq + j, gab + g)),
                  pl.BlockSpec((None, None, None, nc, HEAD_DIM), lambda i, g, j: (i, g, 0, 0, 0)),
                  pl.BlockSpec((None, None, None, nc, HEAD_DIM), lambda i, g, j: (i, g, 1, 0, 0)),
                  pl.BlockSpec((t, HEAD_DIM), lambda i, g, j: (i, 2 * g_n + g)),
                  pl.BlockSpec((t, HEAD_DIM), lambda i, g, j: (i, 3 * g_n + g)),
                  pl.BlockSpec((t, HEAD_DIM), lambda i, g, j: (i, 4 * g_n + g)),
                  pl.BlockSpec((t, HEAD_DIM), lambda i, g, j: (i, 5 * g_n + g)),
                  pl.BlockSpec((ns_rows, nc), lambda i, g, j: (0, 0)),
                  pl.BlockSpec((t, ns_pad), lambda i, g, j: (0, 0))],
        out_specs=pl.BlockSpec((tq, gw), lambda i, g, j: (i * nq + j, g)),
        out_shape=jax.ShapeDtypeStruct((b * t, NSA_W), F32),
        scratch_shapes=[pltpu.VMEM((ns_rows, tq), F32)],
        compiler_params=_params(("parallel", "parallel", "arbitrary"), 48),
        name="nsa_prompt",
    )(qn, qr, z, ckv, ckv, kv16, kv16, kv16, kv16, jnp.asarray(ovt, BF16), jnp.asarray(et, BF16))


def _sb_body(q_ref, k_ref, v_ref, u_ref, o_ref, *, tq, tk, nh):
    j = pl.program_id(2)
    t_col = j * tq + _iota((tq, 1), 0)
    qs = [q_ref[:, h * HEAD_DIM:(h + 1) * HEAD_DIM].astype(BF16) for h in range(nh)]
    u = u_ref[...]
    n_tiles = (j + 1) * (tq // tk)

    def live(carry):
        i, _, _, c_max = carry
        return jnp.logical_and(i < n_tiles, c_max > -SB_UNDERFLOW)

    def step(carry):
        i, cs, accs, _ = carry
        k0 = pl.multiple_of((n_tiles - 1 - i) * tk, tk)
        mask = k0 + _iota((1, tk), 1) < t_col
        new_cs, new_accs = [], []
        for h in range(nh):
            sl = slice(h * HEAD_DIM, (h + 1) * HEAD_DIM)
            k = k_ref[pl.ds(k0, tk), sl].astype(BF16)
            v = v_ref[pl.ds(k0, tk), sl].astype(BF16)
            z = _dot_nt(qs[h], k)
            sp = _softplus(z)
            log1m = jnp.where(mask, -sp, 0.0)
            hi, lo = _split2(log1m)
            suffix = _dot(hi, u) + _dot(lo, u)
            a = jnp.where(mask, jnp.exp(z - sp + cs[h] + suffix - log1m), 0.0)
            new_cs.append(cs[h] + suffix[:, 0:1])
            new_accs.append(accs[h] + _dot(a.astype(BF16), v))
        c_max = functools.reduce(jnp.maximum, [jnp.max(c) for c in new_cs])
        return i + 1, tuple(new_cs), tuple(new_accs), c_max

    init = (jnp.int32(0), tuple(jnp.zeros((tq, 1), F32) for _ in range(nh)),
            tuple(jnp.zeros((tq, HEAD_DIM), F32) for _ in range(nh)), jnp.float32(0.0))
    _, _, accs, _ = lax.while_loop(live, step, init)
    for h in range(nh):
        o_ref[:, h * HEAD_DIM:(h + 1) * HEAD_DIM] = accs[h]


def _sb_prompt(z, b, t):
    tq = 128
    tk = 128
    nh = 4
    nq = t // tq
    gw = nh * HEAD_DIM
    u = (np.arange(tk)[:, None] >= np.arange(tk)[None, :]).astype(np.float32)
    qb, kb, vb = OFF_QB // gw, OFF_KB // gw, OFF_VB // gw
    return pl.pallas_call(
        functools.partial(_sb_body, tq=tq, tk=tk, nh=nh),
        grid=(b, SB_HEADS // nh, nq),
        in_specs=[pl.BlockSpec((tq, gw), lambda i, h, j: (i * nq + j, qb + h)),
                  pl.BlockSpec((t, gw), lambda i, h, j: (i, kb + h)),
                  pl.BlockSpec((t, gw), lambda i, h, j: (i, vb + h)),
                  pl.BlockSpec((tk, tk), lambda i, h, j: (0, 0))],
        out_specs=pl.BlockSpec((tq, gw), lambda i, h, j: (i * nq + j, h)),
        out_shape=jax.ShapeDtypeStruct((b * t, SB_W), F32),
        compiler_params=_params(("parallel", "parallel", "arbitrary"), 48),
        name="sb_prompt",
    )(z, z, z, jnp.asarray(u, BF16))


def _mix_body(oa_ref, ob_ref, za_ref, zb_ref, m0_ref, m1_ref, x_ref, wa_ref, wb_ref, wo_ref, h_ref):
    ua = (oa_ref[...] * _silu(za_ref[...])).astype(BF16)
    ub = (ob_ref[...] * _silu(zb_ref[...])).astype(BF16)
    ya = _dot(ua, wa_ref[...])
    yb = _dot(ub, wb_ref[...])
    mix = jax.nn.sigmoid(m0_ref[...]) * ya + jax.nn.sigmoid(m1_ref[...]) * yb
    h_ref[...] = x_ref[...] + _dot(mix.astype(BF16), wo_ref[...])


def _mix(o_a, o_b, z, x, w_up_a, w_up_b, w_out, l):
    n, d = x.shape
    tm = min(n, 256)
    once = pl.Buffered(1)
    return pl.pallas_call(
        _mix_body,
        grid=(n // tm,),
        in_specs=[pl.BlockSpec((tm, NSA_W), lambda i: (i, 0)),
                  pl.BlockSpec((tm, SB_W), lambda i: (i, 0)),
                  pl.BlockSpec((tm, NSA_W), lambda i: (i, OFF_ZA // NSA_W)),
                  pl.BlockSpec((tm, SB_W), lambda i: (i, OFF_ZB // SB_W)),
                  pl.BlockSpec((tm, d), lambda i: (i, OFF_M0 // d)),
                  pl.BlockSpec((tm, d), lambda i: (i, OFF_M0 // d + 1)),
                  pl.BlockSpec((tm, d), lambda i: (i, 0)),
                  pl.BlockSpec((None, NSA_W, d), lambda i: (l, 0, 0), pipeline_mode=once),
                  pl.BlockSpec((None, SB_W, d), lambda i: (l, 0, 0), pipeline_mode=once),
                  pl.BlockSpec((None, d, d), lambda i: (l, 0, 0), pipeline_mode=once)],
        out_specs=pl.BlockSpec((tm, d), lambda i: (i, 0)),
        out_shape=jax.ShapeDtypeStruct((n, d), F32),
        compiler_params=_params(("parallel",), 56),
        name="mix",
    )(o_a, o_b, z, z, z, z, x, w_up_a, w_up_b, w_out)


def _ple_body(h_ref, p_ref, g_ref, wpg_ref, wple_ref, o_ref):
    h = h_ref[...]
    hn = (h * lax.rsqrt(jnp.mean(h * h, axis=-1, keepdims=True) + RMS_EPS) * g_ref[...]).astype(BF16)
    gate = jax.nn.sigmoid(_dot(hn, wpg_ref[...]))
    o_ref[...] = h + gate * _dot(p_ref[...].astype(BF16), wple_ref[...])


def _ple(h, p, g, w_pg, w_ple, l):
    n, d = h.shape
    pd = p.shape[1]
    tm = min(n, 256)
    once = pl.Buffered(1)
    return pl.pallas_call(
        _ple_body,
        grid=(n // tm,),
        in_specs=[pl.BlockSpec((tm, d), lambda i: (i, 0)),
                  pl.BlockSpec((tm, pd), lambda i: (i, 0)),
                  pl.BlockSpec((1, d), lambda i: (0, 0)),
                  pl.BlockSpec((None, d, d), lambda i: (l, 0, 0), pipeline_mode=once),
                  pl.BlockSpec((None, pd, d), lambda i: (l, 0, 0), pipeline_mode=once)],
        out_specs=pl.BlockSpec((tm, d), lambda i: (i, 0)),
        out_shape=jax.ShapeDtypeStruct((n, d), F32),
        compiler_params=_params(("parallel",), 40),
        name="ple",
    )(h, p, g, w_pg, w_ple)


def _sbdec_body(pt_ref, qm_ref, cache_ref, tri_ref, ex_ref, o_ref, buf_ref, sem_ref, *, l, n_pages):
    i = pl.program_id(0)

    def page_copy(p, slot):
        page = pt_ref[i * n_pages + n_pages - 1 - p]
        return pltpu.make_async_copy(cache_ref.at[l, page], buf_ref.at[slot], sem_ref.at[slot])

    page_copy(0, 0).start()
    tri = tri_ref[...]
    head_lane = _iota((1, LANES), 1) < SB_HEADS

    def live(carry):
        p, _, _, c_max = carry
        return jnp.logical_and(p < n_pages, c_max > -SB_UNDERFLOW)

    def step(carry):
        p, c, acc, _ = carry
        slot = lax.rem(p, 2)
        page_copy(p, slot).wait()

        @pl.when(p + 1 < n_pages)
        def _():
            page_copy(p + 1, 1 - slot).start()

        page = buf_ref.at[slot]
        z = sum(_dot(_rows(page, h, PAGE_SIZE, SB_ROWS).astype(BF16), qm_ref[h * HEAD_DIM:(h + 1) * HEAD_DIM, :])
                for h in range(SB_HEADS))
        sp = _softplus(z)
        log1m = -sp
        hi, lo = _split2(log1m)
        suffix = _dot(tri, hi) + _dot(tri, lo)
        a = jnp.exp(z - sp + c + suffix - log1m)
        a_wide = _dot(a.astype(BF16), ex_ref[...])
        upd = jnp.concatenate(
            [jnp.sum(a_wide[:, h * HEAD_DIM:(h + 1) * HEAD_DIM] * _rows(page, SB_HEADS + h, PAGE_SIZE, SB_ROWS),
                     axis=0, keepdims=True) for h in range(SB_HEADS)], axis=1)
        c = c + suffix[0:1, :]
        c_max = jnp.max(jnp.where(head_lane, c, -jnp.inf))
        return p + 1, c, acc + upd, c_max

    init = (jnp.int32(0), jnp.zeros((1, LANES), F32), jnp.zeros((1, SB_W), F32), jnp.float32(0.0))
    p_end, _, acc, _ = lax.while_loop(live, step, init)

    @pl.when(p_end < n_pages)
    def _():
        page_copy(p_end, lax.rem(p_end, 2)).wait()

    o_ref[...] = acc


def _sb_sample(q_b, cache_sb, page_table, l):
    b, n_pages = page_table.shape
    head_of_row = np.arange(SB_W) // HEAD_DIM
    onehot = (head_of_row[:, None] == np.arange(LANES)[None, :])
    qm = jnp.where(jnp.asarray(onehot)[None], q_b[:, :, None], 0.0).astype(BF16)
    tri = (np.arange(PAGE_SIZE)[:, None] <= np.arange(PAGE_SIZE)[None, :]).astype(np.float32)
    grid_spec = pltpu.PrefetchScalarGridSpec(
        num_scalar_prefetch=1,
        grid=(b,),
        in_specs=[pl.BlockSpec((None, SB_W, LANES), lambda i, pt: (i, 0, 0)),
                  pl.BlockSpec(memory_space=pl.ANY),
                  pl.BlockSpec((PAGE_SIZE, PAGE_SIZE), lambda i, pt: (0, 0)),
                  pl.BlockSpec((LANES, SB_W), lambda i, pt: (0, 0))],
        out_specs=pl.BlockSpec((None, 1, SB_W), lambda i, pt: (i, 0, 0)),
        scratch_shapes=[pltpu.VMEM((2, PAGE_SIZE * SB_ROWS, HEAD_DIM), F32),
                        pltpu.SemaphoreType.DMA((2,))])
    out = pl.pallas_call(
        functools.partial(_sbdec_body, l=l, n_pages=n_pages),
        grid_spec=grid_spec,
        out_shape=jax.ShapeDtypeStruct((b, 1, SB_W), F32),
        compiler_params=_params(("arbitrary",), 32),
        name="sb_sample",
    )(page_table.reshape(-1), qm, cache_sb, jnp.asarray(tri, BF16), jnp.asarray(onehot.T, BF16))
    return out.reshape(b, SB_W)


def _nsadec_cmp_body(*refs, n_pc, n_cmp, n_sel, k_top, t_q):
    pt_ref, qn_ref = refs[0], refs[1]
    page_refs = refs[2:2 + n_pc]
    pos_ref, w1_ref, b1_ref, w2_ref, ov_ref, sel_ref, oc_ref, p_ref, q_ref = refs[2 + n_pc:]
    del pt_ref
    c = pl.program_id(1)
    rows_pp = PAGE_SIZE // CMP_STRIDE
    half = CMP_STRIDE * HEAD_DIM
    n_chunk = n_pc * rows_pp

    def block_rows(page_ref, kg):
        return jnp.concatenate(
            [_rows(page_ref, l * NSA_ROWS + kg, rows_pp, CMP_STRIDE * NSA_ROWS) for l in range(CMP_STRIDE)],
            axis=1)

    for kind in range(2):
        r = jnp.concatenate([block_rows(page_refs[i], kind * NSA_KV_GROUPS + g)
                             for g in range(NSA_KV_GROUPS) for i in range(n_pc)], axis=0)
        p = _dot((r + pos_ref[kind, 0:1, :]).astype(BF16), w1_ref[kind, 0:half, :])
        q = _dot((r + pos_ref[kind, 1:2, :]).astype(BF16), w1_ref[kind, half:2 * half, :])
        r0 = pl.multiple_of(c * n_chunk, n_chunk)
        for g in range(NSA_KV_GROUPS):
            p_ref[kind, g, pl.ds(r0, n_chunk), :] = p[g * n_chunk:(g + 1) * n_chunk]
            q_ref[kind, g, pl.ds(r0, n_chunk), :] = q[g * n_chunk:(g + 1) * n_chunk]

    @pl.when(c == pl.num_programs(1) - 1)
    def _():
        nc = p_ref.shape[2]
        nsp = ov_ref.shape[1]
        qn = qn_ref[...]
        head_group = _iota((NSA_HEADS, 1), 0) // HEADS_PER_GROUP
        n_idx = _iota((1, nc), 1)
        cmask = (n_idx * CMP_STRIDE + (CMP_BLOCK - 1) <= t_q) & (n_idx < n_cmp)
        blk = _iota((1, nsp), 1)
        o_c = jnp.zeros((NSA_HEADS, HEAD_DIM), F32)
        sel_rows = jnp.zeros((NSA_HEADS, nsp), F32)
        for g in range(NSA_KV_GROUPS):
            ckv = []
            for kind in range(2):
                hid = _cmp_hidden(p_ref[kind, g], q_ref[kind, g], b1_ref[kind])
                ckv.append(_dot(hid.astype(BF16), w2_ref[kind]).astype(BF16))
            p = _masked_softmax(_dot_nt(qn, ckv[0]), cmask)
            mine = head_group == g
            o_c = jnp.where(mine, _dot(p.astype(BF16), ckv[1]), o_c)
            psum = jnp.sum(jnp.where(mine, p, 0.0), axis=0, keepdims=True)
            psum = jnp.broadcast_to(psum, (SUBLANES, nc))
            imp = sum(_dot(part, ov_ref[...]) for part in _split3(psum))[0:1, :]
            score = _block_scores(imp, blk, t_q, n_sel)
            along_lanes = jnp.broadcast_to(score, (nsp, nsp))
            along_rows = along_lanes.T
            beats = jnp.where(_iota((nsp, nsp), 0) < _iota((nsp, nsp), 1),
                              jnp.where(along_rows >= along_lanes, 1.0, 0.0),
                              jnp.where(along_rows > along_lanes, 1.0, 0.0))
            rank = jnp.sum(beats, axis=0, keepdims=True)
            chosen = jnp.where(rank < k_top, 1.0, 0.0)
            sel_rows = jnp.where(_iota((NSA_HEADS, 1), 0) == g, chosen, sel_rows)
        sel_ref[...] = sel_rows
        oc_ref[...] = o_c


def _nsa_sample_select(qn, cache_nsa, page_table, pos, w1, b1, w2, l, past_len):
    b, n_pages = page_table.shape
    n_pc = min(8, n_pages)
    t_all = past_len + 1
    nc = past_len // CMP_STRIDE
    ns_pad = -(-(-(-t_all // SEL_BLOCK)) // LANES) * LANES
    n_cmp, n_sel, ovt = _sel_geometry(t_all, nc, ns_pad)
    k_top = min(N_SELECT, n_sel)
    half = CMP_STRIDE * HEAD_DIM
    kernel = functools.partial(_nsadec_cmp_body, n_pc=n_pc, n_cmp=n_cmp, n_sel=n_sel, k_top=k_top,
                               t_q=past_len)

    def page_spec(i_pc):
        return pl.BlockSpec((None, None, PAGE_SIZE * NSA_ROWS, HEAD_DIM),
                            lambda i, c, pt: (l, pt[i * n_pages + c * n_pc + i_pc], 0, 0))

    grid_spec = pltpu.PrefetchScalarGridSpec(
        num_scalar_prefetch=1,
        grid=(b, n_pages // n_pc),
        in_specs=[pl.BlockSpec((None, NSA_HEADS, HEAD_DIM), lambda i, c, pt: (i, 0, 0))]
        + [page_spec(i_pc) for i_pc in range(n_pc)]
        + [pl.BlockSpec((None, 2, 2, half), lambda i, c, pt: (l, 0, 0, 0)),
           pl.BlockSpec((None, 2, 2 * half, HEAD_DIM), lambda i, c, pt: (l, 0, 0, 0)),
           pl.BlockSpec((None, 2, 1, HEAD_DIM), lambda i, c, pt: (l, 0, 0, 0)),
           pl.BlockSpec((None, 2, HEAD_DIM, HEAD_DIM), lambda i, c, pt: (l, 0, 0, 0)),
           pl.BlockSpec((nc, ns_pad), lambda i, c, pt: (0, 0))],
        out_specs=[pl.BlockSpec((None, NSA_HEADS, ns_pad), lambda i, c, pt: (i, 0, 0)),
                   pl.BlockSpec((None, NSA_HEADS, HEAD_DIM), lambda i, c, pt: (i, 0, 0))],
        scratch_shapes=[pltpu.VMEM((2, NSA_KV_GROUPS, nc, HEAD_DIM), F32),
                        pltpu.VMEM((2, NSA_KV_GROUPS, nc, HEAD_DIM), F32)])
    sel, o_c = pl.pallas_call(
        kernel,
        grid_spec=grid_spec,
        out_shape=[jax.ShapeDtypeStruct((b, NSA_HEADS, ns_pad), F32),
                   jax.ShapeDtypeStruct((b, NSA_HEADS, HEAD_DIM), F32)],
        compiler_params=_params(("parallel", "arbitrary"), 48),
        name="nsa_sample_select",
    )(page_table.reshape(-1), qn, *([cache_nsa] * n_pc), pos, w1, b1, w2,
      jnp.asarray(ovt.T, BF16))
    return sel, o_c, n_sel, k_top


def _nsadec_att_body(*refs, k_top, n_sel):
    idx_ref, pt_ref, qr_ref, gate_ref, oc_ref = refs[:5]
    blk_refs = refs[5:5 + k_top]
    new_ref, wnew_ref, win_ref, o_ref = refs[5 + k_top:]
    del pt_ref
    i = pl.program_id(0)
    g = pl.program_id(1)
    q = qr_ref[...]
    qf = q.astype(F32)
    base = (i * NSA_KV_GROUPS + g) * k_top
    g_n = NSA_KV_GROUPS

    def of_group(ref, kind, count, stride):
        if count == 1:
            return jnp.where(g == 0, ref[kind * g_n:kind * g_n + 1, :], ref[kind * g_n + 1:kind * g_n + 2, :])
        return jnp.where(g == 0, _rows(ref, kind * g_n, count, stride), _rows(ref, kind * g_n + 1, count, stride))

    k_all = jnp.concatenate([of_group(r, 2, SEL_BLOCK, NSA_ROWS) for r in blk_refs], axis=0).astype(BF16)
    v_all = jnp.concatenate([of_group(r, 3, SEL_BLOCK, NSA_ROWS) for r in blk_refs], axis=0).astype(BF16)
    slot = _iota((1, k_top * SEL_BLOCK), 1) >> (SEL_BLOCK.bit_length() - 1)
    cached = jnp.zeros((1, k_top * SEL_BLOCK), F32)
    n_new = jnp.int32(0)
    for s in range(k_top):
        is_new = idx_ref[base + s] == n_sel - 1
        cached = jnp.where(slot == s, jnp.where(is_new, 0.0, 1.0), cached)
        n_new = n_new + is_new.astype(jnp.int32)
    valid = cached > 0.5
    has_new = n_new > 0
    sc = jnp.where(valid, _dot_nt(q, k_all), NEG_BIG)
    s_new = jnp.sum(qf * of_group(new_ref, 2, 1, 1), axis=-1, keepdims=True)
    s_new = jnp.where(has_new, s_new, NEG_BIG)
    m_s = jnp.maximum(jnp.max(sc, axis=-1, keepdims=True), s_new)
    p = jnp.where(valid, jnp.exp(sc - m_s), 0.0)
    p_new = jnp.where(has_new, jnp.exp(s_new - m_s), 0.0)
    l_s = jnp.sum(p, axis=-1, keepdims=True) + p_new
    o_s = (_dot(p.astype(BF16), v_all) + p_new * of_group(new_ref, 3, 1, 1)) / jnp.where(l_s > 0.0, l_s, 1.0)

    w_buf = win_ref.shape[0] // WIN_ROWS
    sw = _dot_nt(q, of_group(win_ref, 0, w_buf, WIN_ROWS).astype(BF16))
    wmask = (w_buf - _iota((1, w_buf), 1)) < WINDOW
    sw = jnp.where(wmask, sw, NEG_BIG)
    sw_new = jnp.sum(qf * of_group(wnew_ref, 0, 1, 1), axis=-1, keepdims=True)
    mw = jnp.maximum(jnp.max(sw, axis=-1, keepdims=True), sw_new)
    pw = jnp.where(wmask, jnp.exp(sw - mw), 0.0)
    pw_new = jnp.exp(sw_new - mw)
    l_w = jnp.sum(pw, axis=-1, keepdims=True) + pw_new
    o_w = (_dot(pw.astype(BF16), of_group(win_ref, 1, w_buf, WIN_ROWS).astype(BF16))
           + pw_new * of_group(wnew_ref, 1, 1, 1)) / l_w

    gates = jax.nn.sigmoid(gate_ref[...])
    o_ref[...] = gates[0] * oc_ref[...] + gates[1] * o_s + gates[2] * o_w


def _nsa_sample_attend(idx, qr, gates, o_c, cache_nsa, page_table, nsa_new, win_new, state_win, l,
                       n_sel, k_top):
    b, n_pages = page_table.shape
    per_page = PAGE_SIZE // SEL_BLOCK
    last = n_sel - 2
    g_n = NSA_KV_GROUPS
    blk_rows = SEL_BLOCK * NSA_ROWS
    win_rows = state_win.shape[2]

    def blk_map(i, g, idx_ref, pt, *, s):
        blk = jnp.minimum(idx_ref[(i * g_n + g) * k_top + s], last)
        return (l, pt[i * n_pages + blk // per_page], blk % per_page, 0, 0)

    grid_spec = pltpu.PrefetchScalarGridSpec(
        num_scalar_prefetch=2,
        grid=(b, g_n),
        in_specs=[pl.BlockSpec((None, NSA_HEADS, HEAD_DIM), lambda i, g, *_: (i, 0, 0)),
                  pl.BlockSpec((None, 3, NSA_HEADS, HEAD_DIM), lambda i, g, *_: (i, 0, 0, 0)),
                  pl.BlockSpec((None, NSA_HEADS, HEAD_DIM), lambda i, g, *_: (i, 0, 0))]
        + [pl.BlockSpec((None, None, None, blk_rows, HEAD_DIM), functools.partial(blk_map, s=s))
           for s in range(k_top)]
        + [pl.BlockSpec((None, NSA_ROWS, HEAD_DIM), lambda i, g, *_: (i, 0, 0)),
           pl.BlockSpec((None, WIN_ROWS, HEAD_DIM), lambda i, g, *_: (i, 0, 0)),
           pl.BlockSpec((None, None, win_rows, HEAD_DIM), lambda i, g, *_: (l, i, 0, 0))],
        out_specs=pl.BlockSpec((None, None, NSA_HEADS, HEAD_DIM), lambda i, g, *_: (i, g, 0, 0)))
    return pl.pallas_call(
        functools.partial(_nsadec_att_body, k_top=k_top, n_sel=n_sel),
        grid_spec=grid_spec,
        out_shape=jax.ShapeDtypeStruct((b, g_n, NSA_HEADS, HEAD_DIM), F32),
        compiler_params=_params(("parallel", "parallel"), 32),
        name="nsa_sample_attend",
    )(idx.reshape(-1), page_table.reshape(-1), qr, gates, o_c, *([cache_nsa] * k_top), nsa_new, win_new,
      state_win)


def _permute_w_in(w_in, d_model):
    o = np.cumsum([0, NSA_W, 6 * KV_W, 3 * NSA_HEADS, NSA_W, 3 * SB_W, SB_W, 2 * d_model])
    depth = w_in.shape[0]
    gate = w_in[:, :, o[2]:o[3]]
    per_group = 3 * HEADS_PER_GROUP
    gate_cols = []
    for g in range(NSA_KV_GROUPS):
        gate_cols.append(gate[:, :, g * per_group:(g + 1) * per_group])
        gate_cols.append(jnp.zeros((depth, d_model, LANES - per_group), w_in.dtype))
    gate_cols.append(jnp.zeros((depth, d_model, GA_W - NSA_KV_GROUPS * LANES), w_in.dtype))
    cols = [w_in[:, :, o[0]:o[1]], w_in[:, :, o[3]:o[4]], w_in[:, :, o[5]:o[6]],
            w_in[:, :, o[4]:o[4] + SB_W] * SCALE, w_in[:, :, o[4] + SB_W:o[5]],
            w_in[:, :, o[6]:o[7]], w_in[:, :, o[1]:o[2]]] + gate_cols
    return jnp.concatenate([c.astype(BF16) for c in cols], axis=-1)


def kernel(x_prompt, x_sample, cache_nsa, cache_sb, state_win, page_table, p_prompt, p_sample, g_mix,
           w_in, g_q, g_k, cmp_pos, w_c1, b_c1, w_c2, w_up_a, w_up_b, w_out, g_ple, w_pg, w_ple):
    depth = w_in.shape[0]
    bp, t, d = x_prompt.shape
    bs, ts, _ = x_sample.shape
    assert ts == 1
    n_pages = page_table.shape[1]
    past_len = n_pages * PAGE_SIZE
    n_pool = cache_nsa.shape[1]
    w_buf = state_win.shape[2]
    off_m1, off_kv, off_ga, n_cols = _offsets(d)
    half = CMP_STRIDE * HEAD_DIM

    w_proj = _permute_w_in(w_in, d)
    wa, wb, wo = w_up_a.astype(BF16), w_up_b.astype(BF16), w_out.astype(BF16)
    wpg, wple = w_pg.astype(BF16), w_ple.astype(BF16)
    pos2 = cmp_pos.reshape(depth, 2, 2, half)
    w1 = w_c1.reshape(depth, 2, 2 * half, HEAD_DIM).astype(BF16)
    b1 = b_c1.reshape(depth, 2, 1, HEAD_DIM)
    w2 = w_c2.astype(BF16)
    cos_p, sin_p = _rope_tables(jnp.arange(t))
    cos_s, sin_s = _rope_tables(jnp.full((bs,), past_len))
    nsa_pages = cache_nsa.reshape(depth, n_pool, PAGE_SIZE * NSA_ROWS, HEAD_DIM)
    nsa_blocks = cache_nsa.reshape(depth, n_pool, PAGE_SIZE // SEL_BLOCK, SEL_BLOCK * NSA_ROWS, HEAD_DIM)
    sb_pages = cache_sb.reshape(depth, n_pool, PAGE_SIZE * SB_ROWS, HEAD_DIM)
    win_state = state_win.reshape(depth, bs, w_buf * WIN_ROWS, HEAD_DIM)

    hp = x_prompt.reshape(bp * t, d)
    hs = x_sample.reshape(bs, d)
    outs = [[] for _ in range(6)]
    for l in range(depth):
        gm, gq, gp = g_mix[l][None], g_q[l][None], g_ple[l][None]

        z = _proj(hp, gm, w_proj, l)
        qn, qr, kv16, cmp_in, nsa_rows, win_rows = _prep(z, cos_p, sin_p, gq, g_k[l], off_kv)
        ckv = _compress(cmp_in, bp, t, pos2, w1, b1, w2, l)
        o_a = _nsa_prompt(qn, qr, z, ckv, kv16, bp, t, off_ga)
        o_b = _sb_prompt(z, bp, t)
        h1 = _mix(o_a, o_b, z, hp, wa, wb, wo, l)
        hp = _ple(h1, p_prompt[l].reshape(bp * t, -1), gp, wpg, wple, l)
        outs[0].append(nsa_rows.reshape(bp, t, 4, NSA_KV_GROUPS, HEAD_DIM))
        outs[2].append(z[:, OFF_KB:OFF_KB + 2 * SB_W].reshape(bp, t, 2, SB_HEADS, HEAD_DIM))
        keep = min(WINDOW, t)
        outs[4].append(win_rows.reshape(bp, t, 2, NSA_KV_GROUPS, HEAD_DIM)[:, t - keep:])

        zs = _proj(hs, gm, w_proj, l)
        qn_s, qr_s, _, _, nsa_new, win_new = _prep(zs, cos_s, sin_s, gq, g_k[l], off_kv)
        sel, o_c, n_sel, k_top = _nsa_sample_select(
            qn_s.reshape(bs, NSA_HEADS, HEAD_DIM), nsa_pages, page_table, pos2, w1, b1, w2, l, past_len)
        _, idx = lax.top_k(sel[:, :NSA_KV_GROUPS, :], k_top)
        gate_logits = zs[:, off_ga:off_ga + NSA_KV_GROUPS * LANES].reshape(bs, NSA_KV_GROUPS, LANES)
        gate_logits = gate_logits[:, :, :3 * HEADS_PER_GROUP].reshape(bs, NSA_HEADS, 3)
        gate_logits = jnp.broadcast_to(gate_logits.transpose(0, 2, 1)[..., None], (bs, 3, NSA_HEADS, HEAD_DIM))
        o_full = _nsa_sample_attend(idx.astype(jnp.int32), qr_s.reshape(bs, NSA_HEADS, HEAD_DIM), gate_logits,
                                    o_c, nsa_blocks, page_table, nsa_new.reshape(bs, NSA_ROWS, HEAD_DIM),
                                    win_new.reshape(bs, WIN_ROWS, HEAD_DIM), win_state, l, n_sel, k_top)
        o_a_s = jnp.concatenate(
            [o_full[:, g, g * HEADS_PER_GROUP:(g + 1) * HEADS_PER_GROUP] for g in range(NSA_KV_GROUPS)],
            axis=1).reshape(bs, NSA_W)
        o_b_s = _sb_sample(zs[:, OFF_QB:OFF_QB + SB_W], sb_pages, page_table, l)
        h1s = _mix(o_a_s, o_b_s, zs, hs, wa, wb, wo, l)
        hs = _ple(h1s, p_sample[l].reshape(bs, -1), gp, wpg, wple, l)
        outs[1].append(nsa_new.reshape(bs, 1, 4, NSA_KV_GROUPS, HEAD_DIM))
        outs[3].append(zs[:, OFF_KB:OFF_KB + 2 * SB_W].reshape(bs, 1, 2, SB_HEADS, HEAD_DIM))
        new_win = jnp.concatenate([win_state[l], win_new.reshape(bs, WIN_ROWS, HEAD_DIM)], axis=1)[:, WIN_ROWS:]
        outs[5].append(new_win.reshape(bs, w_buf, 2, NSA_KV_GROUPS, HEAD_DIM))

    return (hp.reshape(bp, t, d), hs.reshape(bs, 1, d), jnp.stack(outs[0]), jnp.stack(outs[1]),
            jnp.stack(outs[2]), jnp.stack(outs[3]), jnp.stack(outs[4]), jnp.stack(outs[5]))
```
